```python
import math
import jax
import jax.numpy as jnp
from jax import lax
import numpy as np

D_MODEL = 1024
BATCH = 2
SEQ = 8192
DEPTH = 1
DEC_BATCH = 32
DEC_SEQ = 2048
PAST_LEN = 128

D_RNN = 1280
LRU_BLOCKS = 10
LRU_BW = D_RNN // LRU_BLOCKS
CONV_W = 4
LRU_C = 8.0
GLA_HEADS = 4
GLA_DK = 128
GLA_DV = 256
GLA_DK_TOT = GLA_HEADS * GLA_DK
GLA_DV_TOT = GLA_HEADS * GLA_DV
GLA_RANK = 16
GLA_TAU = 16.0
GLA_CHUNK = 64
N_EXPERTS = 64
N_GROUPS = 8
TOPK_GROUPS = 4
TOP_K = 8
D_EXPERT = 256
D_SHARED = 256
ROUTED_SCALE = 2.5
MOE_BLOCK = 128
EPS = 1e-6

IN_SIZES = (D_RNN, D_RNN, GLA_DK_TOT, GLA_DK_TOT, GLA_DV_TOT, GLA_DV_TOT, 2 * GLA_RANK, D_MODEL, D_MODEL)
N_IN = sum(IN_SIZES)

kernel_name = "hybrid_rglru_gla_moe_adaln_encoder"


def _rmsnorm(x, g):
    xf = x.astype(jnp.float32)
    y = xf * lax.rsqrt(jnp.mean(xf * xf, axis=-1, keepdims=True) + EPS)
    return (y * g.astype(jnp.float32)).astype(x.dtype)


def _split_in(z):
    pts, acc = [], 0
    for sz in IN_SIZES[:-1]:
        acc += sz
        pts.append(acc)
    return jnp.split(z, pts, axis=-1)


def _centred_conv(x, w, b):
    s = x.shape[1]
    left = CONV_W // 2
    right = CONV_W - 1 - left
    xp = jnp.pad(x, ((0, 0), (left, right), (0, 0)))
    out = b
    for i in range(CONV_W):
        out = out + xp[:, i:i + s, :] * w[i]
    return out


def _lin_combine(left, right):
    a1, b1 = left
    a2, b2 = right
    return a1 * a2, a2 * b1 + b2


def _rglru(xc, wa, ba, wi, bi, lam, reverse):
    bsz, s, _ = xc.shape
    xf = xc.astype(jnp.float32)
    xb = xf.reshape(bsz, s, LRU_BLOCKS, LRU_BW)
    r = jax.nn.sigmoid(jnp.einsum("bshi,hij->bshj", xb, wa.astype(jnp.float32)).reshape(bsz, s, D_RNN) + ba)
    i = jax.nn.sigmoid(jnp.einsum("bshi,hij->bshj", xb, wi.astype(jnp.float32)).reshape(bsz, s, D_RNN) + bi)
    log_a = -LRU_C * r * jax.nn.softplus(-lam.astype(jnp.float32))
    a = jnp.exp(log_a)
    u = jnp.sqrt(-jnp.expm1(2.0 * log_a)) * (i * xf)
    _, hseq = lax.associative_scan(_lin_combine, (a, u), reverse=reverse, axis=1)
    return hseq


def _gla_causal(q, k, v, g, strict):
    bsz, s, nh, dk = q.shape
    dv = v.shape[-1]
    n = s // GLA_CHUNK

    def to_chunks(t):
        return t.reshape(bsz, n, GLA_CHUNK, nh, t.shape[-1]).transpose(1, 0, 3, 2, 4)

    pos = jnp.arange(GLA_CHUNK)
    mask = (pos[:, None] > pos[None, :]) if strict else (pos[:, None] >= pos[None, :])

    def step(state, inp):
        qc, kc, vc, gc = inp
        b = jnp.cumsum(gc, axis=2)
        b_last = b[:, :, -1:, :]
        q_dec = qc * jnp.exp(b)
        att = jnp.einsum("bhtd,bhsd->bhts", q_dec, kc * jnp.exp(-b))
        att = jnp.where(mask, att, 0.0)
        o = jnp.einsum("bhts,bhsv->bhtv", att, vc) + jnp.einsum("bhtd,bhdv->bhtv", q_dec, state)
        state = state * jnp.exp(b_last)[:, :, 0, :, None] + jnp.einsum("bhsd,bhsv->bhdv", kc * jnp.exp(b_last - b), vc)
        return state, o

    s0 = jnp.zeros((bsz, nh, dk, dv), jnp.float32)
    _, o = lax.scan(step, s0, (to_chunks(q), to_chunks(k), to_chunks(v), to_chunks(g)))
    return o.transpose(1, 0, 3, 2, 4).reshape(bsz, s, nh, dv)


def _mixer(h, w_in, conv_w, conv_b, lru_wa, lru_ba, lru_wi, lru_bi, lru_lambda,
           gla_w2, gla_b2, gla_norm_g, proj_lru, proj_gla, merge_b, w_out):
    bsz, s, _ = h.shape
    f32 = jnp.float32
    xr, yr, q, k, v, r, lr, gl, gg = _split_in(h @ w_in)
    xc = _centred_conv(xr, conv_w, conv_b)
    h_lru = (_rglru(xc, lru_wa[0], lru_ba[0], lru_wi[0], lru_bi[0], lru_lambda[0], False)
             + _rglru(xc, lru_wa[1], lru_ba[1], lru_wi[1], lru_bi[1], lru_lambda[1], True))
    y_lru = (h_lru * jax.nn.gelu(yr.astype(f32))).astype(h.dtype)
    shp_k = (bsz, s, GLA_HEADS, GLA_DK)
    qh = (q.astype(f32) * GLA_DK ** -0.5).reshape(shp_k)
    kh = k.astype(f32).reshape(shp_k)
    vh = v.astype(f32).reshape(bsz, s, GLA_HEADS, GLA_DV)
    lr_f, lr_b = jnp.split(lr.astype(f32), 2, axis=-1)
    g_f = (jax.nn.log_sigmoid(lr_f @ gla_w2[0].astype(f32) + gla_b2[0]) / GLA_TAU).reshape(shp_k)
    g_b = (jax.nn.log_sigmoid(lr_b @ gla_w2[1].astype(f32) + gla_b2[1]) / GLA_TAU).reshape(shp_k)
    o_f = _gla_causal(qh, kh, vh, g_f, False)
    flip = lambda t: jnp.flip(t, axis=1)
    o_b = flip(_gla_causal(flip(qh), flip(kh), flip(vh), flip(g_b), True))
    o = o_f + o_b
    o = o * lax.rsqrt(jnp.mean(o * o, axis=-1, keepdims=True) + EPS) * gla_norm_g.astype(f32).reshape(GLA_HEADS, GLA_DV)
    y_gla = (o.reshape(bsz, s, GLA_DV_TOT) * jax.nn.silu(r.astype(f32))).astype(h.dtype)
    merged = (jax.nn.sigmoid(gl + merge_b[0]) * (y_lru @ proj_lru)
              + jax.nn.sigmoid(gg + merge_b[1]) * (y_gla @ proj_gla))
    return merged @ w_out


def _moe(h, router_w, router_bias, w_gate, w_up, w_down, sh_gate, sh_up, sh_down):
    t, d = h.shape
    scores = jax.nn.sigmoid((h @ router_w).astype(jnp.float32))
    sel = scores + router_bias
    grp = sel.reshape(t, N_GROUPS, N_EXPERTS // N_GROUPS)
    grp_score = lax.top_k(grp, 2)[0].sum(-1)
    _, top_g = lax.top_k(grp_score, TOPK_GROUPS)
    gmask = jax.nn.one_hot(top_g, N_GROUPS, dtype=jnp.float32).sum(1) > 0
    emask = jnp.repeat(gmask, N_EXPERTS // N_GROUPS, axis=1)
    _, top_e = lax.top_k(jnp.where(emask, sel, -jnp.inf), TOP_K)
    w = jnp.take_along_axis(scores, top_e, axis=1)
    w = w / jnp.sum(w, axis=-1, keepdims=True) * ROUTED_SCALE
    a_tot = t * TOP_K
    e_flat = top_e.reshape(-1)
    tok_flat = jnp.repeat(jnp.arange(t, dtype=jnp.int32), TOP_K)
    w_flat = w.reshape(-1)
    order = jnp.argsort(e_flat)
    e_s, tok_s, w_s = e_flat[order], tok_flat[order], w_flat[order]
    counts = jnp.bincount(e_flat, length=N_EXPERTS)
    padded = (counts + MOE_BLOCK - 1) // MOE_BLOCK * MOE_BLOCK
    start = jnp.cumsum(counts) - counts
    pend = jnp.cumsum(padded)
    pstart = pend - padded
    dest = pstart[e_s] + (jnp.arange(a_tot) - start[e_s])
    n_rows = ((a_tot + MOE_BLOCK - 1) // MOE_BLOCK + N_EXPERTS) * MOE_BLOCK
    n_blocks = n_rows // MOE_BLOCK
    row_tok = jnp.full((n_rows,), t, jnp.int32).at[dest].set(tok_s)
    row_w = jnp.zeros((n_rows,), jnp.float32).at[dest].set(w_s)
    blk_expert = jnp.minimum(jnp.searchsorted(pend, jnp.arange(n_blocks) * MOE_BLOCK, side="right"), N_EXPERTS - 1)
    hp = jnp.concatenate([h, jnp.zeros((1, d), h.dtype)], axis=0)

    def body(acc, inp):
        rows, rw, e = inp
        xb = hp[rows]
        act = jax.nn.silu(xb @ w_gate[e]) * (xb @ w_up[e])
        yb = (act @ w_down[e]).astype(jnp.float32) * rw[:, None]
        return acc.at[rows].add(yb), None

    acc0 = jnp.zeros((t + 1, d), jnp.float32)
    acc, _ = lax.scan(body, acc0, (row_tok.reshape(n_blocks, MOE_BLOCK), row_w.reshape(n_blocks, MOE_BLOCK), blk_expert))
    shared = (jax.nn.silu(h @ sh_gate) * (h @ sh_up)) @ sh_down
    return (acc[:t] + shared.astype(jnp.float32)).astype(h.dtype)


def _layer(x, c, ada_w, ada_b, norm1_g, w_in, conv_w, conv_b, lru_wa, lru_ba, lru_wi, lru_bi, lru_lambda,
           gla_w2, gla_b2, gla_norm_g, proj_lru, proj_gla, merge_b, w_out, norm2_g, router_w, router_bias,
           exp_w_gate, exp_w_up, exp_w_down, sh_w_gate, sh_w_up, sh_w_down):
    bsz, s, d = x.shape
    mod = jax.nn.silu(c) @ ada_w + ada_b
    sh1, sc1, g1, sh2, sc2, g2 = [m[:, None, :] for m in jnp.split(mod, 6, axis=-1)]
    h = _rmsnorm(x, norm1_g) * (1 + sc1) + sh1
    x = x + g1 * _mixer(h, w_in, conv_w, conv_b, lru_wa, lru_ba, lru_wi, lru_bi, lru_lambda,
                        gla_w2, gla_b2, gla_norm_g, proj_lru, proj_gla, merge_b, w_out)
    h = _rmsnorm(x, norm2_g) * (1 + sc2) + sh2
    y = _moe(h.reshape(bsz * s, d), router_w, router_bias, exp_w_gate, exp_w_up, exp_w_down,
             sh_w_gate, sh_w_up, sh_w_down).reshape(bsz, s, d)
    return x + g2 * y


def setup_inputs(seed: int = 0) -> dict:
    key = jax.random.key(seed)
    ks = iter(jax.random.split(key, 40))
    f32 = jnp.float32

    def nrm(shape, scale):
        return jax.random.normal(next(ks), shape, f32) * scale

    L = DEPTH
    lam_u = jax.random.uniform(next(ks), (L, 2, D_RNN), f32, minval=0.9, maxval=0.999)
    return {
        "x_prompt": nrm((BATCH, SEQ, D_MODEL), 1.0),
        "x_sample": nrm((DEC_BATCH, DEC_SEQ, D_MODEL), 1.0),
        "c_prompt": nrm((BATCH, D_MODEL), 1.0),
        "c_sample": nrm((DEC_BATCH, D_MODEL), 1.0),
        "ada_w": nrm((L, D_MODEL, 6 * D_MODEL), 0.5 * D_MODEL ** -0.5),
        "ada_b": nrm((L, 6 * D_MODEL), 0.02),
        "norm1_g": 1.0 + nrm((L, D_MODEL), 0.02),
        "w_in": nrm((L, D_MODEL, N_IN), D_MODEL ** -0.5),
        "conv_w": nrm((L, CONV_W, D_RNN), CONV_W ** -0.5),
        "conv_b": nrm((L, D_RNN), 0.02),
        "lru_wa": nrm((L, 2, LRU_BLOCKS, LRU_BW, LRU_BW), LRU_BW ** -0.5),
        "lru_ba": nrm((L, 2, D_RNN), 0.02),
        "lru_wi": nrm((L, 2, LRU_BLOCKS, LRU_BW, LRU_BW), LRU_BW ** -0.5),
        "lru_bi": nrm((L, 2, D_RNN), 0.02),
        "lru_lambda": jnp.log(lam_u) - jnp.log1p(-lam_u),
        "gla_w2": nrm((L, 2, GLA_RANK, GLA_DK_TOT), GLA_RANK ** -0.5),
        "gla_b2": nrm((L, 2, GLA_DK_TOT), 0.02),
        "gla_norm_g": 1.0 + nrm((L, GLA_DV_TOT), 0.02),
        "proj_lru": nrm((L, D_RNN, D_MODEL), D_RNN ** -0.5),
        "proj_gla": nrm((L, GLA_DV_TOT, D_MODEL), GLA_DV_TOT ** -0.5),
        "merge_b": nrm((L, 2, D_MODEL), 0.02),
        "w_out": nrm((L, D_MODEL, D_MODEL), D_MODEL ** -0.5),
        "norm2_g": 1.0 + nrm((L, D_MODEL), 0.02),
        "router_w": nrm((L, D_MODEL, N_EXPERTS), D_MODEL ** -0.5),
        "router_bias": nrm((L, N_EXPERTS), 0.01),
        "exp_w_gate": nrm((L, N_EXPERTS, D_MODEL, D_EXPERT), D_MODEL ** -0.5),
        "exp_w_up": nrm((L, N_EXPERTS, D_MODEL, D_EXPERT), D_MODEL ** -0.5),
        "exp_w_down": nrm((L, N_EXPERTS, D_EXPERT, D_MODEL), D_EXPERT ** -0.5),
        "sh_w_gate": nrm((L, D_MODEL, D_SHARED), D_MODEL ** -0.5),
        "sh_w_up": nrm((L, D_MODEL, D_SHARED), D_MODEL ** -0.5),
        "sh_w_down": nrm((L, D_SHARED, D_MODEL), D_SHARED ** -0.5),
        "final_g": 1.0 + nrm((D_MODEL,), 0.02),
    }


def reference(x_prompt, x_sample, c_prompt, c_sample, ada_w, ada_b, norm1_g, w_in, conv_w, conv_b,
              lru_wa, lru_ba, lru_wi, lru_bi, lru_lambda, gla_w2, gla_b2, gla_norm_g, proj_lru, proj_gla,
              merge_b, w_out, norm2_g, router_w, router_bias, exp_w_gate, exp_w_up, exp_w_down,
              sh_w_gate, sh_w_up, sh_w_down, final_g):
    layer_params = (ada_w, ada_b, norm1_g, w_in, conv_w, conv_b, lru_wa, lru_ba, lru_wi, lru_bi, lru_lambda,
                    gla_w2, gla_b2, gla_norm_g, proj_lru, proj_gla, merge_b, w_out, norm2_g, router_w,
                    router_bias, exp_w_gate, exp_w_up, exp_w_down, sh_w_gate, sh_w_up, sh_w_down)

    def trunk(x, c):
        for l in range(DEPTH):
            x = _layer(x, c, *[p[l] for p in layer_params])
        return _rmsnorm(x, final_g)

    y_prompt = trunk(x_prompt, c_prompt)
    y_sample = trunk(x_sample, c_sample)
    return (y_prompt, y_sample)
```

```python
import functools

import jax
import jax.numpy as jnp
from jax import lax
from jax.experimental import pallas as pl
from jax.experimental.pallas import tpu as pltpu

F32 = jnp.float32
BF16 = jnp.bfloat16
I32 = jnp.int32

EPS = 1e-6
LRU_C = 8.0
GLA_HEADS = 4
GLA_TAU = 16.0
GLA_CHUNK = 64
N_GROUPS = 8
TOPK_GROUPS = 4
TOP_K = 8
ROUTED_SCALE = 2.5

LANES = 128
SUBLANES = 8
VMEM_LIMIT = 56 * 1024 * 1024

TM_INPROJ = 512
TM_MERGE = 256
TM_ROWS = 128
EXPERT_BLOCK = 256
LRU_CHUNK = 256
CONV_PAD = 16


def _dot(a, b):
    return jnp.dot(a, b, preferred_element_type=F32)


def _split_bf16(a):
    hi = a.astype(BF16)
    lo = (a - hi.astype(F32)).astype(BF16)
    return hi, lo


def _dot3(a, b):
    a_hi, a_lo = _split_bf16(a)
    b_hi, b_lo = _split_bf16(b)
    return _dot(a_hi, b_hi) + _dot(a_lo, b_hi) + _dot(a_hi, b_lo)


def _sigmoid(x):
    return 1.0 / (1.0 + jnp.exp(-x))


def _silu(x):
    return x * _sigmoid(x)


def _softplus(z):
    return jnp.maximum(z, 0.0) + jnp.log(1.0 + jnp.exp(-jnp.abs(z)))


def _gelu_tanh(x):
    return 0.5 * x * (1.0 + jnp.tanh(0.7978845608028654 * (x + 0.044715 * (x * x * x))))


def _rms(x):
    return x * lax.rsqrt(jnp.mean(x * x, axis=-1, keepdims=True) + EPS)


def _const_spec(shape):
    zeros = (0,) * len(shape)
    return pl.BlockSpec(shape, lambda *_: zeros, pipeline_mode=pl.Buffered(1))


def _params(sem, vmem=VMEM_LIMIT):
    return pltpu.CompilerParams(dimension_semantics=sem, vmem_limit_bytes=vmem)


def _ada_kernel(c_ref, w_ref, b_ref, o_ref):
    o_ref[...] = _dot3(_silu(c_ref[...]), w_ref[...]) + b_ref[...]


def _ada(c, ada_w, ada_b):
    nb, d = c.shape
    nbp = -(-nb // SUBLANES) * SUBLANES
    cp = jnp.pad(c, ((0, nbp - nb), (0, 0)))
    n6 = ada_w.shape[1] // d
    out = pl.pallas_call(
        _ada_kernel,
        out_shape=jax.ShapeDtypeStruct((nbp, n6 * d), F32),
        grid=(n6,),
        in_specs=[pl.BlockSpec((nbp, d), lambda j: (0, 0)),
                  pl.BlockSpec((d, d), lambda j: (0, j)),
                  pl.BlockSpec((1, d), lambda j: (0, j))],
        out_specs=pl.BlockSpec((nbp, d), lambda j: (0, j)),
        compiler_params=_params(("arbitrary",)),
        name="ada",
    )(cp, ada_w, ada_b.reshape(1, -1))
    return out.reshape(nbp, n6, d)


def _inproj_kernel(x_ref, mod_ref, g_ref, *refs):
    n = len(refs) // 2
    w_refs, o_refs = refs[:n], refs[n:]
    h = _rms(x_ref[...]) * g_ref[...]
    h = h * (1.0 + mod_ref[0, 1:2, :]) + mod_ref[0, 0:1, :]
    hb = h.astype(BF16)
    for w_ref, o_ref in zip(w_refs, o_refs):
        o_ref[...] = _dot(hb, w_ref[...]).astype(o_ref.dtype)


def _inproj(x2, mod, norm_g, w_parts, seq):
    t, d = x2.shape
    tm = TM_INPROJ
    per_seq = seq // tm
    n6 = mod.shape[1]
    in_specs = [pl.BlockSpec((tm, d), lambda i: (i, 0)),
                pl.BlockSpec((1, n6, d), lambda i: (i // per_seq, 0, 0)),
                _const_spec((1, d))]
    in_specs += [_const_spec(w.shape) for w in w_parts]
    out_shape = [jax.ShapeDtypeStruct((t, w.shape[1]), BF16) for w in w_parts]
    out_specs = [pl.BlockSpec((tm, w.shape[1]), lambda i: (i, 0)) for w in w_parts]
    return pl.pallas_call(
        _inproj_kernel,
        out_shape=out_shape,
        grid=(t // tm,),
        in_specs=in_specs,
        out_specs=out_specs,
        compiler_params=_params(("arbitrary",)),
        name="inproj",
    )(x2, mod, norm_g.reshape(1, d), *w_parts)


def _scan_chunk(a, u, h, reverse, store):
    rows = a.shape[0]
    rm = lax.broadcasted_iota(I32, a.shape, 0) & (SUBLANES - 1)
    for d in (1, 2, 4):
        if reverse:
            a_s = pltpu.roll(a, rows - d, 0)
            u_s = pltpu.roll(u, rows - d, 0)
            keep = rm < SUBLANES - d
        else:
            a_s = pltpu.roll(a, d, 0)
            u_s = pltpu.roll(u, d, 0)
            keep = rm >= d
        u = a * jnp.where(keep, u_s, 0.0) + u
        a = a * jnp.where(keep, a_s, 1.0)
    groups = rows // SUBLANES
    order = range(groups - 1, -1, -1) if reverse else range(groups)
    for g in order:
        sl = slice(g * SUBLANES, (g + 1) * SUBLANES)
        hg = u[sl] + a[sl] * h
        store(g, hg)
        h = hg[0:1] if reverse else hg[SUBLANES - 1:SUBLANES]
    return h


def _lru_kernel(xr_ref, yr_ref, cw_ref, cb_ref, wa_ref, wi_ref, ba_ref, bi_ref, lam_ref,
                o_ref, xpad, xc, hf, hb):
    s = xr_ref.shape[1]
    ch = LRU_CHUNK
    n = s // ch
    taps = cw_ref.shape[0]
    left = taps // 2
    zpad = jnp.zeros((CONV_PAD, LANES), F32)
    xpad[0:CONV_PAD, :] = zpad
    xpad[CONV_PAD + s:CONV_PAD + s + CONV_PAD, :] = zpad

    def fill(c, _):
        r0 = pl.multiple_of(c * ch, ch)
        xpad[pl.ds(CONV_PAD + r0, ch), :] = xr_ref[0, pl.ds(r0, ch), :].astype(F32)
        return 0
    lax.fori_loop(0, n, fill, 0)

    def conv(c, _):
        r0 = pl.multiple_of(c * ch, ch)
        win = xpad[pl.ds(r0, ch + 2 * CONV_PAD), :]
        acc = jnp.zeros_like(win) + cb_ref[...]
        for i in range(taps):
            sh = (left - i) % (ch + 2 * CONV_PAD)
            src = win if sh == 0 else pltpu.roll(win, sh, 0)
            acc = acc + src * cw_ref[i:i + 1, :]
        xc[pl.ds(r0, ch), :] = acc[CONV_PAD:CONV_PAD + ch]
        return 0
    lax.fori_loop(0, n, conv, 0)

    def gates(r0, d):
        x = xc[pl.ds(r0, ch), :]
        xb = x.astype(BF16)
        r = _sigmoid(_dot(xb, wa_ref[d, 0]) + ba_ref[d:d + 1, :])
        i = _sigmoid(_dot(xb, wi_ref[d, 0]) + bi_ref[d:d + 1, :])
        log_a = (-LRU_C * _softplus(-lam_ref[d:d + 1, :])) * r
        a = jnp.exp(log_a)
        u = jnp.sqrt(1.0 - a * a) * (i * x)
        return a, u

    def step(c, carry):
        h_f, h_b = carry
        rf = pl.multiple_of(c * ch, ch)
        rb = pl.multiple_of((n - 1 - c) * ch, ch)
        a, u = gates(rf, 0)

        def store_f(g, v):
            hf[pl.ds(rf + g * SUBLANES, SUBLANES), :] = v
        h_f = _scan_chunk(a, u, h_f, False, store_f)
        a, u = gates(rb, 1)

        def store_b(g, v):
            hb[pl.ds(rb + g * SUBLANES, SUBLANES), :] = v
        h_b = _scan_chunk(a, u, h_b, True, store_b)
        return h_f, h_b
    zero = jnp.zeros((1, LANES), F32)
    lax.fori_loop(0, n, step, (zero, zero))

    def finish(c, _):
        r0 = pl.multiple_of(c * ch, ch)
        y = yr_ref[0, pl.ds(r0, ch), :].astype(F32)
        o_ref[0, pl.ds(r0, ch), :] = ((hf[pl.ds(r0, ch), :] + hb[pl.ds(r0, ch), :])
                                      * _gelu_tanh(y)).astype(o_ref.dtype)
        return 0
    lax.fori_loop(0, n, finish, 0)


def _lru(xr, yr, conv_w, conv_b, wa, wi, ba, bi, lam):
    b, s, c = xr.shape
    nblk = c // LANES
    seq = lambda: pl.BlockSpec((1, s, LANES), lambda i, j: (i, 0, j))
    chan = lambda rows: pl.BlockSpec((rows, LANES), lambda i, j: (0, j))
    wspec = lambda: pl.BlockSpec((2, 1, LANES, LANES), lambda i, j: (0, j, 0, 0))
    return pl.pallas_call(
        _lru_kernel,
        out_shape=jax.ShapeDtypeStruct((b, s, c), BF16),
        grid=(b, nblk),
        in_specs=[seq(), seq(), chan(conv_w.shape[0]), chan(1), wspec(), wspec(),
                  chan(2), chan(2), chan(2)],
        out_specs=seq(),
        scratch_shapes=[pltpu.VMEM((s + 2 * CONV_PAD, LANES), F32),
                        pltpu.VMEM((s, LANES), F32),
                        pltpu.VMEM((s, LANES), F32),
                        pltpu.VMEM((s, LANES), F32)],
        compiler_params=_params(("arbitrary", "arbitrary")),
        name="lru",
    )(xr, yr, conv_w, conv_b.reshape(1, c), wa, wi, ba, bi, lam)


def _gla_kernel(q_ref, k_ref, v_ref, r_ref, lr_ref, w2_ref, b2_ref, gn_ref, o_ref, acc):
    s = q_ref.shape[1]
    dk = q_ref.shape[2]
    dv = v_ref.shape[2]
    ck = GLA_CHUNK
    n = s // ck
    half = n // 2
    scale = dk ** -0.5
    row = lax.broadcasted_iota(I32, (ck, ck), 0)
    col = lax.broadcasted_iota(I32, (ck, ck), 1)
    tri_f = (row >= col).astype(BF16)
    tri_b = (row <= col).astype(BF16)
    mask_f = row >= col
    mask_b = col > row
    lane2 = lax.broadcasted_iota(I32, (2 * ck, 2 * ck), 1)
    zeros_v = jnp.zeros((ck, dv), BF16)

    def chunk(c, state, d):
        r0 = pl.multiple_of(c * ck, ck)
        q = q_ref[0, pl.ds(r0, ck), :].astype(F32) * scale
        k = k_ref[0, pl.ds(r0, ck), :].astype(F32)
        v = v_ref[0, pl.ds(r0, ck), :]
        lr = lr_ref[0, pl.ds(r0, ck), :]
        pre = _dot(lr, w2_ref[d]) + b2_ref[d:d + 1, :]
        g = -_softplus(-pre) * (1.0 / GLA_TAU)
        g_hi, g_lo = _split_bf16(g)
        tri = tri_b if d else tri_f
        b = _dot(tri, g_hi) + _dot(tri, g_lo)
        b_edge = b[0:1] if d else b[ck - 1:ck]
        q_dec = (q * jnp.exp(b)).astype(BF16)
        k_dec = (k * jnp.exp(-b)).astype(BF16)
        att = lax.dot_general(q_dec, k_dec, (((1,), (1,)), ((), ())), preferred_element_type=F32)
        att = jnp.where(mask_b if d else mask_f, att, 0.0)
        o = _dot(att.astype(BF16), v) + _dot(q_dec, state.astype(BF16))
        k_st = k * jnp.exp(b_edge - b)
        stacked_t = jnp.concatenate([k_st, b], axis=0).T
        edge_lane = ck if d else 2 * ck - 1
        decay = jnp.exp(stacked_t[:, edge_lane:edge_lane + 1])
        k_t = jnp.where(lane2 < ck, stacked_t, 0.0).astype(BF16)
        state = state * decay + _dot(k_t, jnp.concatenate([v, zeros_v], axis=0))
        return o, state

    def finish(r0, o):
        y = _rms(o) * gn_ref[...]
        rr = r_ref[0, pl.ds(r0, ck), :].astype(F32)
        o_ref[0, pl.ds(r0, ck), :] = (y * _silu(rr)).astype(o_ref.dtype)

    def first(i, carry):
        s_f, s_b = carry
        cb = n - 1 - i
        o, s_f = chunk(i, s_f, 0)
        acc[pl.ds(pl.multiple_of(i * ck, ck), ck), :] = o
        o, s_b = chunk(cb, s_b, 1)
        acc[pl.ds(pl.multiple_of(cb * ck, ck), ck), :] = o
        return s_f, s_b

    def second(i, carry):
        s_f, s_b = carry
        cb = n - 1 - i
        rf = pl.multiple_of(i * ck, ck)
        rb = pl.multiple_of(cb * ck, ck)
        o, s_f = chunk(i, s_f, 0)
        finish(rf, o + acc[pl.ds(rf, ck), :])
        o, s_b = chunk(cb, s_b, 1)
        finish(rb, o + acc[pl.ds(rb, ck), :])
        return s_f, s_b

    zero = jnp.zeros((dk, dv), F32)
    carry = lax.fori_loop(0, half, first, (zero, zero))
    lax.fori_loop(half, n, second, carry)


def _gla(q, k, v, r, lr, w2p, b2, gnorm):
    b, s, dkt = q.shape
    dvt = v.shape[2]
    h = GLA_HEADS
    dk, dv = dkt // h, dvt // h
    nlr = lr.shape[2]
    kspec = lambda: pl.BlockSpec((1, s, dk), lambda i, j: (i, 0, j))
    vspec = lambda: pl.BlockSpec((1, s, dv), lambda i, j: (i, 0, j))
    return pl.pallas_call(
        _gla_kernel,
        out_shape=jax.ShapeDtypeStruct((b, s, dvt), BF16),
        grid=(b, h),
        in_specs=[kspec(), kspec(), vspec(), vspec(),
                  pl.BlockSpec((1, s, nlr), lambda i, j: (i, 0, 0)),
                  pl.BlockSpec((2, nlr, dk), lambda i, j: (0, 0, j)),
                  pl.BlockSpec((2, dk), lambda i, j: (0, j)),
                  pl.BlockSpec((1, dv), lambda i, j: (0, j))],
        out_specs=vspec(),
        scratch_shapes=[pltpu.VMEM((s, dv), F32)],
        compiler_params=_params(("arbitrary", "arbitrary")),
        name="gla",
    )(q, k, v, r, lr, w2p, b2, gnorm.reshape(1, dvt))


def _group_reduce(x, lane, op):
    for sft in (1, 2, 4):
        up = pltpu.roll(x, LANES - sft, 1)
        dn = pltpu.roll(x, sft, 1)
        x = op(x, jnp.where((lane & sft) == 0, up, dn))
    return x


def _route(scores, sel, n_exp):
    neg = -jnp.inf
    lane = lax.broadcasted_iota(I32, scores.shape, 1)
    lane_f = lane.astype(F32)
    per_group = n_exp // N_GROUPS
    assert per_group == SUBLANES and n_exp <= LANES
    valid = lane < n_exp
    v = jnp.where(valid, sel, neg)
    m1 = _group_reduce(v, lane, jnp.maximum)
    first = _group_reduce(jnp.where(v == m1, lane_f, float(LANES)), lane, jnp.minimum)
    m2 = _group_reduce(jnp.where(lane_f == first, neg, v), lane, jnp.maximum)
    gs = m1 + m2
    gidx = lane >> 3
    n_slots = LANES // per_group
    rank = jnp.zeros(scores.shape, F32)
    for kk in range(1, n_slots):
        other = pltpu.roll(gs, per_group * kk, 1)
        og = (gidx - kk) & (n_slots - 1)
        better = jnp.where(other > gs, 1.0, jnp.where(other == gs, jnp.where(og < gidx, 1.0, 0.0), 0.0))
        rank = rank + better
    w = jnp.where(rank < float(TOPK_GROUPS), v, neg)
    cols = []
    chosen = jnp.zeros(scores.shape, F32)
    for _ in range(TOP_K):
        m = jnp.max(w, axis=1, keepdims=True)
        j = jnp.min(jnp.where(w == m, lane_f, float(LANES)), axis=1, keepdims=True)
        pick = lane_f == j
        w = jnp.where(pick, neg, w)
        chosen = jnp.where(pick, 1.0, chosen)
        cols.append(j)
    return cols, chosen, lane_f


def _merge_kernel(n_exp, x_ref, yl_ref, yg_ref, gl_ref, gg_ref, mod_ref, n2_ref, mb_ref,
                  pl_ref, pg_ref, wo_ref, rw_ref, rb_ref, sg_ref, su_ref, sd_ref,
                  base_ref, h_ref, e_ref, p_ref, w_ref, cnt_ref, run):
    tm = x_ref.shape[0]

    @pl.when(pl.program_id(0) == 0)
    def _():
        run[...] = jnp.zeros_like(run)

    a = _dot(yl_ref[...], pl_ref[...])
    b = _dot(yg_ref[...], pg_ref[...])
    merged = (_sigmoid(gl_ref[...].astype(F32) + mb_ref[0:1, :]) * a
              + _sigmoid(gg_ref[...].astype(F32) + mb_ref[1:2, :]) * b)
    mix = _dot(merged.astype(BF16), wo_ref[...])
    x1 = x_ref[...] + mod_ref[0, 2:3, :] * mix
    h = _rms(x1) * n2_ref[...]
    h = h * (1.0 + mod_ref[0, 4:5, :]) + mod_ref[0, 3:4, :]
    h_ref[...] = h
    hb = h.astype(BF16)
    shared = _dot(( _silu(_dot(hb, sg_ref[...])) * _dot(hb, su_ref[...]) ).astype(BF16), sd_ref[...])
    base_ref[...] = x1 + mod_ref[0, 5:6, :] * shared

    logits = _dot3(h, rw_ref[...])
    scores = _sigmoid(logits)
    cols, chosen, lane_f = _route(scores, scores + rb_ref[...], n_exp)
    picked = jnp.where(chosen > 0.0, scores, 0.0)
    cw = picked * (ROUTED_SCALE / jnp.sum(picked, axis=1, keepdims=True))

    r = lax.broadcasted_iota(I32, (tm, tm), 0)
    c = lax.broadcasted_iota(I32, (tm, tm), 1)
    before = (c < r).astype(BF16)
    pos = _dot(before, chosen.astype(BF16)) + run[...]
    run[...] = pos[tm - 1:tm, :] + chosen[tm - 1:tm, :]
    cnt_ref[...] = run[...].astype(I32)

    e_out = jnp.zeros(scores.shape, F32)
    p_out = jnp.zeros(scores.shape, F32)
    w_out = jnp.zeros(scores.shape, F32)
    for kk, j in enumerate(cols):
        hit = lane_f == j
        pk = jnp.sum(jnp.where(hit, pos, 0.0), axis=1, keepdims=True)
        wk = jnp.sum(jnp.where(hit, cw, 0.0), axis=1, keepdims=True)
        slot = lane_f == float(kk)
        e_out = jnp.where(slot, j, e_out)
        p_out = jnp.where(slot, pk, p_out)
        w_out = jnp.where(slot, wk, w_out)
    e_ref[...] = e_out.astype(I32)
    p_ref[...] = p_out.astype(I32)
    w_ref[...] = w_out


def _merge(x2, y_lru, y_gla, gl, gg, mod, norm2_g, merge_b, proj_lru, proj_gla, w_out,
           router_w, router_bias, sh_gate, sh_up, sh_down, seq):
    t, d = x2.shape
    tm = TM_MERGE
    per_seq = seq // tm
    n6 = mod.shape[1]
    n_exp = router_w.shape[1]
    rw = jnp.pad(router_w, ((0, 0), (0, LANES - n_exp)))
    rb = jnp.pad(router_bias.reshape(1, n_exp), ((0, 0), (0, LANES - n_exp)))
    tok = lambda width: pl.BlockSpec((tm, width), lambda i: (i, 0))
    in_specs = [tok(d), tok(y_lru.shape[1]), tok(y_gla.shape[1]), tok(d), tok(d),
                pl.BlockSpec((1, n6, d), lambda i: (i // per_seq, 0, 0)),
                _const_spec((1, d)), _const_spec((2, d)),
                _const_spec(proj_lru.shape), _const_spec(proj_gla.shape), _const_spec(w_out.shape),
                _const_spec(rw.shape), _const_spec(rb.shape),
                _const_spec(sh_gate.shape), _const_spec(sh_up.shape), _const_spec(sh_down.shape)]
    out_shape = [jax.ShapeDtypeStruct((t, d), F32), jax.ShapeDtypeStruct((t, d), F32),
                 jax.ShapeDtypeStruct((t, LANES), I32), jax.ShapeDtypeStruct((t, LANES), I32),
                 jax.ShapeDtypeStruct((t, LANES), F32), jax.ShapeDtypeStruct((1, LANES), I32)]
    out_specs = [tok(d), tok(d), tok(LANES), tok(LANES), tok(LANES),
                 pl.BlockSpec((1, LANES), lambda i: (0, 0))]
    return pl.pallas_call(
        functools.partial(_merge_kernel, n_exp),
        out_shape=out_shape,
        grid=(t // tm,),
        in_specs=in_specs,
        out_specs=out_specs,
        scratch_shapes=[pltpu.VMEM((1, LANES), F32)],
        compiler_params=_params(("arbitrary",)),
        name="merge",
    )(x2, y_lru, y_gla, gl, gg, mod, norm2_g.reshape(1, d), merge_b, proj_lru, proj_gla, w_out,
      rw, rb, sh_gate, sh_up, sh_down)


def _row_copy(src, dst, sem):
    return pltpu.make_async_copy(src, dst, sem)


def _dispatch_kernel(pstart_ref, e_ref, p_ref, h_ref, xs_in, xs_ref, sem):
    del xs_in
    tn = h_ref.shape[0]

    def issue(t, _):
        for kk in range(TOP_K):
            i = t * TOP_K + kk
            dst = pstart_ref[e_ref[0, 0, i]] + p_ref[0, 0, i]
            _row_copy(h_ref.at[pl.ds(t, 1)], xs_ref.at[pl.ds(dst, 1)], sem).start()
        return 0
    lax.fori_loop(0, tn, issue, 0)

    def drain(t, _):
        for kk in range(TOP_K):
            _row_copy(h_ref.at[pl.ds(0, 1)], xs_ref.at[pl.ds(0, 1)], sem).wait()
        return 0
    lax.fori_loop(0, tn, drain, 0)


def _dispatch(pstart, e_flat, p_flat, h, n_rows):
    t, d = h.shape
    tn = TM_ROWS
    steps = t // tn
    idx = lambda: pl.BlockSpec((1, 1, tn * TOP_K), lambda i, ps: (i, 0, 0), memory_space=pltpu.SMEM)
    xs0 = jnp.zeros((n_rows, d), F32)
    grid_spec = pltpu.PrefetchScalarGridSpec(
        num_scalar_prefetch=1,
        grid=(steps,),
        in_specs=[idx(), idx(), pl.BlockSpec((tn, d), lambda i, ps: (i, 0)),
                  pl.BlockSpec(memory_space=pl.ANY)],
        out_specs=pl.BlockSpec(memory_space=pl.ANY),
        scratch_shapes=[pltpu.SemaphoreType.DMA(())],
    )
    return pl.pallas_call(
        _dispatch_kernel,
        out_shape=jax.ShapeDtypeStruct((n_rows, d), F32),
        grid_spec=grid_spec,
        input_output_aliases={4: 0},
        compiler_params=_params(("arbitrary",)),
        name="dispatch",
    )(pstart, e_flat.reshape(steps, 1, tn * TOP_K), p_flat.reshape(steps, 1, tn * TOP_K), h, xs0)


def _experts_kernel(be_ref, xs_ref, wg_ref, wu_ref, wd_ref, ys_ref):
    del be_ref
    x = xs_ref[...].astype(BF16)
    act = _silu(_dot(x, wg_ref[0])) * _dot(x, wu_ref[0])
    ys_ref[...] = _dot(act.astype(BF16), wd_ref[0])


def _experts(blk_expert, xs, wg, wu, wd):
    n_rows, d = xs.shape
    blk = EXPERT_BLOCK
    de = wg.shape[2]
    grid_spec = pltpu.PrefetchScalarGridSpec(
        num_scalar_prefetch=1,
        grid=(n_rows // blk,),
        in_specs=[pl.BlockSpec((blk, d), lambda i, be: (i, 0)),
                  pl.BlockSpec((1, d, de), lambda i, be: (be[i], 0, 0)),
                  pl.BlockSpec((1, d, de), lambda i, be: (be[i], 0, 0)),
                  pl.BlockSpec((1, de, d), lambda i, be: (be[i], 0, 0))],
        out_specs=pl.BlockSpec((blk, d), lambda i, be: (i, 0)),
    )
    return pl.pallas_call(
        _experts_kernel,
        out_shape=jax.ShapeDtypeStruct((n_rows, d), F32),
        grid_spec=grid_spec,
        compiler_params=_params(("arbitrary",)),
        name="experts",
    )(blk_expert, xs, wg, wu, wd)


def _combine_kernel(pstart_ref, e_ref, p_ref, base_ref, w_ref, mod_ref, fg_ref, ys_ref,
                    o_ref, buf, sem):
    tn = base_ref.shape[0]

    def issue(t, _):
        for kk in range(TOP_K):
            i = t * TOP_K + kk
            src = pstart_ref[e_ref[0, 0, i]] + p_ref[0, 0, i]
            _row_copy(ys_ref.at[pl.ds(src, 1)], buf.at[pl.ds(kk * tn + t, 1)], sem).start()
        return 0
    lax.fori_loop(0, tn, issue, 0)

    def drain(t, _):
        for kk in range(TOP_K):
            _row_copy(ys_ref.at[pl.ds(0, 1)], buf.at[pl.ds(0, 1)], sem).wait()
        return 0
    lax.fori_loop(0, tn, drain, 0)

    w = w_ref[...]
    acc = jnp.zeros(base_ref.shape, F32)
    for kk in range(TOP_K):
        acc = acc + w[:, kk:kk + 1] * buf[kk * tn:(kk + 1) * tn, :]
    y = base_ref[...] + mod_ref[0, 5:6, :] * acc
    o_ref[...] = _rms(y) * fg_ref[...]


def _combine(pstart, e_flat, p_flat, base, w8, mod, final_g, ys, seq):
    t, d = base.shape
    tn = TM_ROWS
    steps = t // tn
    per_seq = seq // tn
    n6 = mod.shape[1]
    idx = lambda: pl.BlockSpec((1, 1, tn * TOP_K), lambda i, ps: (i, 0, 0), memory_space=pltpu.SMEM)
    grid_spec = pltpu.PrefetchScalarGridSpec(
        num_scalar_prefetch=1,
        grid=(steps,),
        in_specs=[idx(), idx(),
                  pl.BlockSpec((tn, d), lambda i, ps: (i, 0)),
                  pl.BlockSpec((tn, LANES), lambda i, ps: (i, 0)),
                  pl.BlockSpec((1, n6, d), lambda i, ps: (i // per_seq, 0, 0)),
                  pl.BlockSpec((1, d), lambda i, ps: (0, 0)),
                  pl.BlockSpec(memory_space=pl.ANY)],
        out_specs=pl.BlockSpec((tn, d), lambda i, ps: (i, 0)),
        scratch_shapes=[pltpu.VMEM((TOP_K * tn, d), F32), pltpu.SemaphoreType.DMA(())],
    )
    return pl.pallas_call(
        _combine_kernel,
        out_shape=jax.ShapeDtypeStruct((t, d), F32),
        grid_spec=grid_spec,
        compiler_params=_params(("arbitrary",)),
        name="combine",
    )(pstart, e_flat.reshape(steps, 1, tn * TOP_K), p_flat.reshape(steps, 1, tn * TOP_K),
      base, w8, mod, final_g.reshape(1, d), ys)


def _prepare(l, ada_w, ada_b, norm1_g, w_in, conv_w, conv_b, lru_wa, lru_ba, lru_wi, lru_bi,
             lru_lambda, gla_w2, gla_b2, gla_norm_g, proj_lru, proj_gla, merge_b, w_out, norm2_g,
             router_w, router_bias, exp_w_gate, exp_w_up, exp_w_down, sh_w_gate, sh_w_up,
             sh_w_down):
    d = w_in.shape[1]
    d_rnn = conv_w.shape[2]
    dkt = gla_w2.shape[3]
    dvt = gla_norm_g.shape[1]
    rank = gla_w2.shape[2]
    sizes = (d_rnn, d_rnn, dkt, dkt, dvt, dvt, 2 * rank, d, d)
    wb = w_in[l].astype(BF16)
    parts, off = [], 0
    for sz in sizes:
        parts.append(wb[:, off:off + sz])
        off += sz
    w2 = gla_w2[l].astype(BF16)
    zeros = jnp.zeros_like(w2[0])
    w2p = jnp.stack([jnp.concatenate([w2[0], zeros], axis=0),
                     jnp.concatenate([zeros, w2[1]], axis=0)])
    return dict(
        ada_w=ada_w[l], ada_b=ada_b[l], norm1_g=norm1_g[l], w_parts=parts,
        conv_w=conv_w[l], conv_b=conv_b[l], wa=lru_wa[l].astype(BF16), wi=lru_wi[l].astype(BF16),
        ba=lru_ba[l], bi=lru_bi[l], lam=lru_lambda[l], w2p=w2p, b2=gla_b2[l],
        gnorm=gla_norm_g[l], proj_lru=proj_lru[l].astype(BF16), proj_gla=proj_gla[l].astype(BF16),
        merge_b=merge_b[l], w_out=w_out[l].astype(BF16), norm2_g=norm2_g[l],
        router_w=router_w[l], router_bias=router_bias[l],
        wg=exp_w_gate[l].astype(BF16), wu=exp_w_up[l].astype(BF16), wd=exp_w_down[l].astype(BF16),
        sg=sh_w_gate[l].astype(BF16), su=sh_w_up[l].astype(BF16), sd=sh_w_down[l].astype(BF16))


def _layer(x, c, p, final_g):
    bsz, s, d = x.shape
    t = bsz * s
    x2 = x.reshape(t, d)
    mod = _ada(c, p["ada_w"], p["ada_b"])
    xr, yr, q, k, v, r, lr, gl, gg = _inproj(x2, mod, p["norm1_g"], p["w_parts"], s)
    seqv = lambda a: a.reshape(bsz, s, a.shape[1])
    y_lru = _lru(seqv(xr), seqv(yr), p["conv_w"], p["conv_b"], p["wa"], p["wi"], p["ba"], p["bi"],
                 p["lam"]).reshape(t, -1)
    y_gla = _gla(seqv(q), seqv(k), seqv(v), seqv(r), seqv(lr), p["w2p"], p["b2"],
                 p["gnorm"]).reshape(t, -1)
    base, h, e_out, p_out, w8, counts = _merge(
        x2, y_lru, y_gla, gl, gg, mod, p["norm2_g"], p["merge_b"], p["proj_lru"], p["proj_gla"],
        p["w_out"], p["router_w"], p["router_bias"], p["sg"], p["su"], p["sd"], s)
    n_exp = p["router_w"].shape[1]
    blk = EXPERT_BLOCK
    n_blocks = t * TOP_K // blk + n_exp
    cnt = counts[0, :n_exp]
    padded = (cnt + blk - 1) // blk * blk
    pend = jnp.cumsum(padded)
    pstart = (pend - padded).astype(I32)
    blk_expert = jnp.minimum(
        jnp.sum((pend[None, :] <= (jnp.arange(n_blocks, dtype=I32) * blk)[:, None]).astype(I32), axis=1),
        n_exp - 1).astype(I32)
    e_flat = e_out[:, :TOP_K].reshape(-1)
    p_flat = p_out[:, :TOP_K].reshape(-1)
    xs = _dispatch(pstart, e_flat, p_flat, h, n_blocks * blk)
    ys = _experts(blk_expert, xs, p["wg"], p["wu"], p["wd"])
    return _combine(pstart, e_flat, p_flat, base, w8, mod, final_g, ys, s).reshape(bsz, s, d)


def kernel(x_prompt, x_sample, c_prompt, c_sample, ada_w, ada_b, norm1_g, w_in, conv_w, conv_b,
           lru_wa, lru_ba, lru_wi, lru_bi, lru_lambda, gla_w2, gla_b2, gla_norm_g, proj_lru,
           proj_gla, merge_b, w_out, norm2_g, router_w, router_bias, exp_w_gate, exp_w_up,
           exp_w_down, sh_w_gate, sh_w_up, sh_w_down, final_g):
    depth = ada_w.shape[0]
    assert depth == 1, "the fused final RMSNorm assumes a single layer"
    p = _prepare(0, ada_w, ada_b, norm1_g, w_in, conv_w, conv_b, lru_wa, lru_ba, lru_wi, lru_bi,
                 lru_lambda, gla_w2, gla_b2, gla_norm_g, proj_lru, proj_gla, merge_b, w_out,
                 norm2_g, router_w, router_bias, exp_w_gate, exp_w_up, exp_w_down, sh_w_gate,
                 sh_w_up, sh_w_down)
    return (_layer(x_prompt, c_prompt, p, final_g), _layer(x_sample, c_sample, p, final_g))
```

```python
import functools

import jax
import jax.numpy as jnp
from jax import lax
from jax.experimental import pallas as pl
from jax.experimental.pallas import tpu as pltpu

F32 = jnp.float32
BF16 = jnp.bfloat16
I32 = jnp.int32

EPS = 1e-6
LOG2_E = 1.4426950408889634
LRU_C = 8.0
GLA_HEADS = 4
GLA_TAU = 16.0
GLA_CHUNK = 64
N_GROUPS = 8
TOPK_GROUPS = 4
TOP_K = 8
ROUTED_SCALE = 2.5

LANES = 128
SUBLANES = 8
VMEM_LIMIT = 56 * 1024 * 1024

TM_INPROJ = 512
TM_MERGE = 256
EXPERT_BLOCK = 256
LRU_CHUNK = 256
CONV_PAD = 16


def _dot(a, b):
    return jnp.dot(a, b, preferred_element_type=F32)


def _split_bf16(a):
    hi = a.astype(BF16)
    lo = (a - hi.astype(F32)).astype(BF16)
    return hi, lo


def _dot3(a, b):
    a_hi, a_lo = _split_bf16(a)
    b_hi, b_lo = _split_bf16(b)
    return _dot(a_hi, b_hi) + _dot(a_lo, b_hi) + _dot(a_hi, b_lo)


def _sigmoid(x):
    return 1.0 / (1.0 + jnp.exp(-x))


def _silu(x):
    return x * _sigmoid(x)


def _softplus(z):
    return jnp.maximum(z, 0.0) + jnp.log(1.0 + jnp.exp(-jnp.abs(z)))


def _gelu_tanh(x):
    return 0.5 * x * (1.0 + jnp.tanh(0.7978845608028654 * (x + 0.044715 * (x * x * x))))


def _rms(x):
    return x * lax.rsqrt(jnp.mean(x * x, axis=-1, keepdims=True) + EPS)


def _const_spec(shape):
    zeros = (0,) * len(shape)
    return pl.BlockSpec(shape, lambda *_: zeros, pipeline_mode=pl.Buffered(1))


def _params(sem, vmem=VMEM_LIMIT):
    return pltpu.CompilerParams(dimension_semantics=sem, vmem_limit_bytes=vmem)


def _ada_kernel(c_ref, w_ref, b_ref, o_ref):
    o_ref[...] = _dot3(_silu(c_ref[...]), w_ref[...]) + b_ref[...]


def _ada(c, ada_w, ada_b):
    nb, d = c.shape
    nbp = -(-nb // SUBLANES) * SUBLANES
    cp = jnp.pad(c, ((0, nbp - nb), (0, 0)))
    n6 = ada_w.shape[1] // d
    out = pl.pallas_call(
        _ada_kernel,
        out_shape=jax.ShapeDtypeStruct((nbp, n6 * d), F32),
        grid=(n6,),
        in_specs=[pl.BlockSpec((nbp, d), lambda j: (0, 0)),
                  pl.BlockSpec((d, d), lambda j: (0, j)),
                  pl.BlockSpec((1, d), lambda j: (0, j))],
        out_specs=pl.BlockSpec((nbp, d), lambda j: (0, j)),
        compiler_params=_params(("arbitrary",)),
        name="ada",
    )(cp, ada_w, ada_b.reshape(1, -1))
    return out.reshape(nbp, n6, d)


def _inproj_kernel(x_ref, mod_ref, g_ref, *refs):
    n = len(refs) // 2
    w_refs, o_refs = refs[:n], refs[n:]
    h = _rms(x_ref[...]) * g_ref[...]
    h = h * (1.0 + mod_ref[0, 1:2, :]) + mod_ref[0, 0:1, :]
    hb = h.astype(BF16)
    for w_ref, o_ref in zip(w_refs, o_refs):
        o_ref[...] = _dot(hb, w_ref[...]).astype(o_ref.dtype)


def _inproj(x2, mod, norm_g, w_parts, seq):
    t, d = x2.shape
    tm = TM_INPROJ
    per_seq = seq // tm
    n6 = mod.shape[1]
    in_specs = [pl.BlockSpec((tm, d), lambda i: (i, 0)),
                pl.BlockSpec((1, n6, d), lambda i: (i // per_seq, 0, 0)),
                _const_spec((1, d))]
    in_specs += [_const_spec(w.shape) for w in w_parts]
    out_shape = [jax.ShapeDtypeStruct((t, w.shape[1]), BF16) for w in w_parts]
    out_specs = [pl.BlockSpec((tm, w.shape[1]), lambda i: (i, 0)) for w in w_parts]
    return pl.pallas_call(
        _inproj_kernel,
        out_shape=out_shape,
        grid=(t // tm,),
        in_specs=in_specs,
        out_specs=out_specs,
        compiler_params=_params(("arbitrary",)),
        name="inproj",
    )(x2, mod, norm_g.reshape(1, d), *w_parts)


def _scan_chunk(a, u, h, reverse, store):
    groups = a.shape[0] // SUBLANES
    a = a.reshape(groups, SUBLANES, LANES)
    u = u.reshape(groups, SUBLANES, LANES)
    rm = lax.broadcasted_iota(I32, a.shape, 1)
    for d in (1, 2, 4):
        shift = SUBLANES - d if reverse else d
        keep = (rm < SUBLANES - d) if reverse else (rm >= d)
        u = a * jnp.where(keep, pltpu.roll(u, shift, 1), 0.0) + u
        a = a * jnp.where(keep, pltpu.roll(a, shift, 1), 1.0)
    order = range(groups - 1, -1, -1) if reverse else range(groups)
    for g in order:
        hg = u[g] + a[g] * h
        store(g, hg)
        h = hg[0:1] if reverse else hg[SUBLANES - 1:SUBLANES]
    return h


def _lru_kernel(xr_ref, yr_ref, cw_ref, cb_ref, wa_ref, wi_ref, ba_ref, bi_ref, lam_ref,
                o_ref, xpad, xc, hf, hb):
    s = xr_ref.shape[1]
    ch = LRU_CHUNK
    n = s // ch
    taps = cw_ref.shape[0]
    left = taps // 2
    zpad = jnp.zeros((CONV_PAD, LANES), F32)
    xpad[0:CONV_PAD, :] = zpad
    xpad[CONV_PAD + s:CONV_PAD + s + CONV_PAD, :] = zpad

    def fill(c, _):
        r0 = pl.multiple_of(c * ch, ch)
        xpad[pl.ds(CONV_PAD + r0, ch), :] = xr_ref[0, pl.ds(r0, ch), :].astype(F32)
        return 0
    lax.fori_loop(0, n, fill, 0)

    def conv(c, _):
        r0 = pl.multiple_of(c * ch, ch)
        win = xpad[pl.ds(r0, ch + 2 * CONV_PAD), :]
        acc = jnp.zeros_like(win) + cb_ref[...]
        for i in range(taps):
            sh = (left - i) % (ch + 2 * CONV_PAD)
            src = win if sh == 0 else pltpu.roll(win, sh, 0)
            acc = acc + src * cw_ref[i:i + 1, :]
        xc[pl.ds(r0, ch), :] = acc[CONV_PAD:CONV_PAD + ch]
        return 0
    lax.fori_loop(0, n, conv, 0)

    def gates(r0, d):
        x = xc[pl.ds(r0, ch), :]
        xb = x.astype(BF16)
        r = _sigmoid(_dot(xb, wa_ref[d, 0]) + ba_ref[d:d + 1, :])
        i = _sigmoid(_dot(xb, wi_ref[d, 0]) + bi_ref[d:d + 1, :])
        a = jnp.exp2((-LRU_C * LOG2_E * _softplus(-lam_ref[d:d + 1, :])) * r)
        y = 1.0 - a * a
        u = jnp.where(y > 0.0, y * lax.rsqrt(y), 0.0) * (i * x)
        return a, u

    def step(c, carry):
        h_f, h_b = carry
        rf = pl.multiple_of(c * ch, ch)
        rb = pl.multiple_of((n - 1 - c) * ch, ch)
        a, u = gates(rf, 0)

        def store_f(g, v):
            hf[pl.ds(rf + g * SUBLANES, SUBLANES), :] = v
        h_f = _scan_chunk(a, u, h_f, False, store_f)
        a, u = gates(rb, 1)

        def store_b(g, v):
            hb[pl.ds(rb + g * SUBLANES, SUBLANES), :] = v
        h_b = _scan_chunk(a, u, h_b, True, store_b)
        return h_f, h_b
    zero = jnp.zeros((1, LANES), F32)
    lax.fori_loop(0, n, step, (zero, zero))

    def finish(c, _):
        r0 = pl.multiple_of(c * ch, ch)
        y = yr_ref[0, pl.ds(r0, ch), :].astype(F32)
        o_ref[0, pl.ds(r0, ch), :] = ((hf[pl.ds(r0, ch), :] + hb[pl.ds(r0, ch), :])
                                      * _gelu_tanh(y)).astype(o_ref.dtype)
        return 0
    lax.fori_loop(0, n, finish, 0)


def _lru(xr, yr, conv_w, conv_b, wa, wi, ba, bi, lam):
    b, s, c = xr.shape
    nblk = c // LANES
    seq = lambda: pl.BlockSpec((1, s, LANES), lambda i, j: (i, 0, j))
    chan = lambda rows: pl.BlockSpec((rows, LANES), lambda i, j: (0, j))
    wspec = lambda: pl.BlockSpec((2, 1, LANES, LANES), lambda i, j: (0, j, 0, 0))
    return pl.pallas_call(
        _lru_kernel,
        out_shape=jax.ShapeDtypeStruct((b, s, c), BF16),
        grid=(b, nblk),
        in_specs=[seq(), seq(), chan(conv_w.shape[0]), chan(1), wspec(), wspec(),
                  chan(2), chan(2), chan(2)],
        out_specs=seq(),
        scratch_shapes=[pltpu.VMEM((s + 2 * CONV_PAD, LANES), F32),
                        pltpu.VMEM((s, LANES), F32),
                        pltpu.VMEM((s, LANES), F32),
                        pltpu.VMEM((s, LANES), F32)],
        compiler_params=_params(("arbitrary", "arbitrary")),
        name="lru",
    )(xr, yr, conv_w, conv_b.reshape(1, c), wa, wi, ba, bi, lam)


def _gla_kernel(q_ref, k_ref, v_ref, r_ref, lr_ref, w2_ref, b2_ref, gn_ref, o_ref, acc):
    s = q_ref.shape[1]
    dk = q_ref.shape[2]
    dv = v_ref.shape[2]
    ck = GLA_CHUNK
    n = s // ck
    half = n // 2
    scale = dk ** -0.5
    row = lax.broadcasted_iota(I32, (ck, ck), 0)
    col = lax.broadcasted_iota(I32, (ck, ck), 1)
    tri_f = (row >= col).astype(BF16)
    tri_b = (row <= col).astype(BF16)
    mask_f = row >= col
    mask_b = col > row
    lane2 = lax.broadcasted_iota(I32, (2 * ck, 2 * ck), 1)
    zeros_v = jnp.zeros((ck, dv), BF16)

    def finish(r0, o):
        y = _rms(o) * gn_ref[...]
        rr = r_ref[0, pl.ds(r0, ck), :].astype(F32)
        o_ref[0, pl.ds(r0, ck), :] = (y * _silu(rr)).astype(o_ref.dtype)

    unroll = max(u for u in (1, 2, 4, 8) if half % u == 0)

    def group(it, carry, final):
        states = list(carry)
        jobs = []
        for uu in range(unroll):
            i = it * unroll + uu
            jobs += [(i, 0), (n - 1 - i, 1)]
        rows = [pl.multiple_of(c * ck, ck) for c, _ in jobs]
        pre = [_dot(lr_ref[0, pl.ds(r0, ck), :], w2_ref[d]) + b2_ref[d:d + 1, :]
               for (_, d), r0 in zip(jobs, rows)]
        logb = []
        for (_, d), x in zip(jobs, pre):
            g_hi, g_lo = _split_bf16(-_softplus(-x) * (1.0 / GLA_TAU))
            tri = tri_b if d else tri_f
            logb.append(_dot(tri, g_hi) + _dot(tri, g_lo))
        q_dec, att, k_t, decay = [], [], [], []
        for (_, d), r0, b in zip(jobs, rows, logb):
            q = q_ref[0, pl.ds(r0, ck), :].astype(F32) * scale
            k = k_ref[0, pl.ds(r0, ck), :].astype(F32)
            qd = (q * jnp.exp(b)).astype(BF16)
            kd = (k * jnp.exp(-b)).astype(BF16)
            a = lax.dot_general(qd, kd, (((1,), (1,)), ((), ())), preferred_element_type=F32)
            b_edge = b[0:1] if d else b[ck - 1:ck]
            stacked_t = jnp.concatenate([k * jnp.exp(b_edge - b), b], axis=0).T
            edge_lane = ck if d else 2 * ck - 1
            q_dec.append(qd)
            att.append(jnp.where(mask_b if d else mask_f, a, 0.0).astype(BF16))
            decay.append(jnp.exp(stacked_t[:, edge_lane:edge_lane + 1]))
            k_t.append(jnp.where(lane2 < ck, stacked_t, 0.0).astype(BF16))
        intra, inject = [], []
        for r0, a, kt in zip(rows, att, k_t):
            v = v_ref[0, pl.ds(r0, ck), :]
            intra.append(_dot(a, v))
            inject.append(_dot(kt, jnp.concatenate([v, zeros_v], axis=0)))
        seen = []
        for (_, d), dc, inj in zip(jobs, decay, inject):
            seen.append(states[d].astype(BF16))
            states[d] = states[d] * dc + inj
        for r0, qd, st, o in zip(rows, q_dec, seen, intra):
            o = o + _dot(qd, st)
            if final:
                finish(r0, o + acc[pl.ds(r0, ck), :])
            else:
                acc[pl.ds(r0, ck), :] = o
        return tuple(states)

    zero = jnp.zeros((dk, dv), F32)
    carry = lax.fori_loop(0, half // unroll, functools.partial(group, final=False), (zero, zero))
    lax.fori_loop(half // unroll, n // unroll, functools.partial(group, final=True), carry)


def _gla(q, k, v, r, lr, w2p, b2, gnorm):
    b, s, dkt = q.shape
    dvt = v.shape[2]
    h = GLA_HEADS
    dk, dv = dkt // h, dvt // h
    nlr = lr.shape[2]
    kspec = lambda: pl.BlockSpec((1, s, dk), lambda i, j: (i, 0, j))
    vspec = lambda: pl.BlockSpec((1, s, dv), lambda i, j: (i, 0, j))
    return pl.pallas_call(
        _gla_kernel,
        out_shape=jax.ShapeDtypeStruct((b, s, dvt), BF16),
        grid=(b, h),
        in_specs=[kspec(), kspec(), vspec(), vspec(),
                  pl.BlockSpec((1, s, nlr), lambda i, j: (i, 0, 0)),
                  pl.BlockSpec((2, nlr, dk), lambda i, j: (0, 0, j)),
                  pl.BlockSpec((2, dk), lambda i, j: (0, j)),
                  pl.BlockSpec((1, dv), lambda i, j: (0, j))],
        out_specs=vspec(),
        scratch_shapes=[pltpu.VMEM((s, dv), F32)],
        compiler_params=_params(("arbitrary", "arbitrary")),
        name="gla",
    )(q, k, v, r, lr, w2p, b2, gnorm.reshape(1, dvt))


def _group_reduce(x, lane, op):
    for sft in (1, 2, 4):
        up = pltpu.roll(x, LANES - sft, 1)
        dn = pltpu.roll(x, sft, 1)
        x = op(x, jnp.where((lane & sft) == 0, up, dn))
    return x


def _route(scores, sel, n_exp):
    neg = -jnp.inf
    lane = lax.broadcasted_iota(I32, scores.shape, 1)
    lane_f = lane.astype(F32)
    per_group = n_exp // N_GROUPS
    assert per_group == SUBLANES and n_exp <= LANES
    valid = lane < n_exp
    v = jnp.where(valid, sel, neg)
    m1 = _group_reduce(v, lane, jnp.maximum)
    first = _group_reduce(jnp.where(v == m1, lane_f, float(LANES)), lane, jnp.minimum)
    m2 = _group_reduce(jnp.where(lane_f == first, neg, v), lane, jnp.maximum)
    gs = m1 + m2
    gidx = lane >> 3
    n_slots = LANES // per_group
    rank = jnp.zeros(scores.shape, F32)
    for kk in range(1, n_slots):
        other = pltpu.roll(gs, per_group * kk, 1)
        og = (gidx - kk) & (n_slots - 1)
        better = jnp.where(other > gs, 1.0, jnp.where(other == gs, jnp.where(og < gidx, 1.0, 0.0), 0.0))
        rank = rank + better
    w = jnp.where(rank < float(TOPK_GROUPS), v, neg)
    cols = []
    chosen = jnp.zeros(scores.shape, F32)
    for _ in range(TOP_K):
        m = jnp.max(w, axis=1, keepdims=True)
        j = jnp.min(jnp.where(w == m, lane_f, float(LANES)), axis=1, keepdims=True)
        pick = lane_f == j
        w = jnp.where(pick, neg, w)
        chosen = jnp.where(pick, 1.0, chosen)
        cols.append(j)
    return cols, chosen, lane_f


def _merge_kernel(n_exp, x_ref, yl_ref, yg_ref, gl_ref, gg_ref, mod_ref, n2_ref, mb_ref,
                  pl_ref, pg_ref, wo_ref, rw_ref, rb_ref, sg_ref, su_ref, sd_ref,
                  base_ref, h_ref, p_ref, w_ref, cnt_ref, ls_ref, rb4_ref, tot_ref, run):
    tm = x_ref.shape[0]

    @pl.when(pl.program_id(0) == 0)
    def _():
        run[...] = jnp.zeros_like(run)

    a = _dot(yl_ref[...], pl_ref[...])
    b = _dot(yg_ref[...], pg_ref[...])
    merged = (_sigmoid(gl_ref[...].astype(F32) + mb_ref[0:1, :]) * a
              + _sigmoid(gg_ref[...].astype(F32) + mb_ref[1:2, :]) * b)
    mix = _dot(merged.astype(BF16), wo_ref[...])
    x1 = x_ref[...] + mod_ref[0, 2:3, :] * mix
    h = _rms(x1) * n2_ref[...]
    h = h * (1.0 + mod_ref[0, 4:5, :]) + mod_ref[0, 3:4, :]
    hb = h.astype(BF16)
    h_ref[...] = hb
    shared = _dot(( _silu(_dot(hb, sg_ref[...])) * _dot(hb, su_ref[...]) ).astype(BF16), sd_ref[...])
    base_ref[...] = x1 + mod_ref[0, 5:6, :] * shared

    logits = _dot3(h, rw_ref[...])
    scores = _sigmoid(logits)
    cols, chosen, lane_f = _route(scores, scores + rb_ref[...], n_exp)
    picked = jnp.where(chosen > 0.0, scores, 0.0)
    cw = picked * (ROUTED_SCALE / jnp.sum(picked, axis=1, keepdims=True))

    r = lax.broadcasted_iota(I32, (tm, tm), 0)
    c = lax.broadcasted_iota(I32, (tm, tm), 1)
    before = (c < r).astype(BF16)
    rank = _dot(before, chosen.astype(BF16))
    cnt = rank[tm - 1:tm, :] + chosen[tm - 1:tm, :]
    cnt8 = jnp.floor((cnt + (SUBLANES - 1)) * (1.0 / SUBLANES)) * SUBLANES
    lr_ = lax.broadcasted_iota(I32, (LANES, LANES), 0)
    lc_ = lax.broadcasted_iota(I32, (LANES, LANES), 1)
    lstart = _dot(jnp.broadcast_to(cnt8, (SUBLANES, LANES)).astype(BF16),
                  (lr_ < lc_).astype(BF16))[0:1]
    pos = rank + lstart
    cnt_ref[0] = cnt8.astype(I32)
    ls_ref[0] = lstart.astype(I32)
    rb4_ref[0] = run[...].astype(I32)
    run[...] = run[...] + cnt8
    tot_ref[...] = run[...].astype(I32)

    p_out = jnp.zeros(scores.shape, F32)
    w_out = jnp.zeros(scores.shape, F32)
    for kk, j in enumerate(cols):
        hit = lane_f == j
        pk = jnp.sum(jnp.where(hit, pos, 0.0), axis=1, keepdims=True)
        wk = jnp.sum(jnp.where(hit, cw, 0.0), axis=1, keepdims=True)
        slot = lane_f == float(kk)
        p_out = jnp.where(slot, pk, p_out)
        w_out = jnp.where(slot, wk, w_out)
    p_ref[...] = p_out.astype(I32)
    w_ref[...] = w_out


def _merge(x2, y_lru, y_gla, gl, gg, mod, norm2_g, merge_b, proj_lru, proj_gla, w_out,
           router_w, router_bias, sh_gate, sh_up, sh_down, seq):
    t, d = x2.shape
    tm = TM_MERGE
    per_seq = seq // tm
    n6 = mod.shape[1]
    n_exp = router_w.shape[1]
    rw = jnp.pad(router_w, ((0, 0), (0, LANES - n_exp)))
    rb = jnp.pad(router_bias.reshape(1, n_exp), ((0, 0), (0, LANES - n_exp)))
    tok = lambda width: pl.BlockSpec((tm, width), lambda i: (i, 0))
    in_specs = [tok(d), tok(y_lru.shape[1]), tok(y_gla.shape[1]), tok(d), tok(d),
                pl.BlockSpec((1, n6, d), lambda i: (i // per_seq, 0, 0)),
                _const_spec((1, d)), _const_spec((2, d)),
                _const_spec(proj_lru.shape), _const_spec(proj_gla.shape), _const_spec(w_out.shape),
                _const_spec(rw.shape), _const_spec(rb.shape),
                _const_spec(sh_gate.shape), _const_spec(sh_up.shape), _const_spec(sh_down.shape)]
    n_tiles = t // tm
    meta = jax.ShapeDtypeStruct((n_tiles, 1, LANES), I32)
    meta_spec = lambda: pl.BlockSpec((1, 1, LANES), lambda i: (i, 0, 0))
    out_shape = [jax.ShapeDtypeStruct((t, d), F32), jax.ShapeDtypeStruct((t, d), BF16),
                 jax.ShapeDtypeStruct((t, LANES), I32), jax.ShapeDtypeStruct((t, LANES), F32),
                 meta, meta, meta, jax.ShapeDtypeStruct((1, LANES), I32)]
    out_specs = [tok(d), tok(d), tok(LANES), tok(LANES), meta_spec(), meta_spec(), meta_spec(),
                 pl.BlockSpec((1, LANES), lambda i: (0, 0))]
    return pl.pallas_call(
        functools.partial(_merge_kernel, n_exp),
        out_shape=out_shape,
        grid=(t // tm,),
        in_specs=in_specs,
        out_specs=out_specs,
        scratch_shapes=[pltpu.VMEM((1, LANES), F32)],
        compiler_params=_params(("arbitrary",)),
        name="merge",
    )(x2, y_lru, y_gla, gl, gg, mod, norm2_g.reshape(1, d), merge_b, proj_lru, proj_gla, w_out,
      rw, rb, sh_gate, sh_up, sh_down)


def _run_copies(n, src_ref, src0, dst_ref, dst0, sem, max_rows, wait=False):
    for b in range(max_rows.bit_length() - 1, 2, -1):
        size = 1 << b

        @pl.when((n & size) != 0)
        def _(b=b, size=size):
            off = (n >> (b + 1)) << (b + 1)
            cp = pltpu.make_async_copy(
                src_ref.at[pl.ds(pl.multiple_of(src0 + off, SUBLANES), size)],
                dst_ref.at[pl.ds(pl.multiple_of(dst0 + off, SUBLANES), size)], sem)
            if wait:
                cp.wait()
            else:
                cp.start()


def _pack(x):
    half = x.shape[1] // 2
    bits = lax.bitcast_convert_type(x.astype(BF16).astype(F32), jnp.uint32)
    return (bits[:, :half] >> 16) | (bits[:, half:] & jnp.uint32(0xFFFF0000))


def _unpack(w):
    lo = lax.bitcast_convert_type(w << 16, F32)
    hi = lax.bitcast_convert_type(w & jnp.uint32(0xFFFF0000), F32)
    return lo.astype(BF16), hi.astype(BF16)


def _stage_rows(tm, n_exp):
    return -(-(TOP_K * tm + SUBLANES * n_exp) // tm) * tm


def _dispatch_kernel(n_exp, pad_ref, cnt_ref, ls_ref, dst_ref, h_ref, p_ref, xs_ref,
                     stage, zbuf, sent, sems):
    i = pl.program_id(0)
    n = pl.num_programs(0)
    tm = h_ref.shape[0]
    rows = stage.shape[1]
    top = 1 << (rows.bit_length() - 1)
    slot = i % 2
    total = ls_ref[0, 0, n_exp]

    def drain(sl):
        _run_copies(sent[sl], stage.at[sl], 0, xs_ref, 0, sems.at[sl], top, wait=True)

    @pl.when(i >= 2)
    def _():
        drain(slot)

    pt = p_ref[...].astype(F32).T

    def sort_rows(c):
        r = (lax.broadcasted_iota(I32, (tm, tm), 0) + c * tm).astype(F32)
        sel = jnp.zeros((tm, tm), F32)
        for kk in range(TOP_K):
            sel = jnp.where(r == pt[kk:kk + 1, :], 1.0, sel)
        stage[slot, c * tm:(c + 1) * tm, :] = _pack(_dot(sel.astype(BF16), h_ref[...]))

    for c in range(rows // tm):
        if c < TOP_K:
            sort_rows(c)
        else:
            pl.when(c * tm < total)(functools.partial(sort_rows, c))

    def per_expert(e, _):
        _run_copies(cnt_ref[0, 0, e], stage.at[slot], ls_ref[0, 0, e], xs_ref, dst_ref[0, 0, e],
                    sems.at[slot], tm)
        return 0
    lax.fori_loop(0, n_exp, per_expert, 0)
    sent[slot] = total

    @pl.when(i == n - 1)
    def _():
        drain(slot)

        @pl.when(n >= 2)
        def _():
            drain(1 - slot)
        blk = zbuf.shape[0]
        zbuf[...] = jnp.zeros_like(zbuf)
        n_blocks = xs_ref.shape[0] // blk

        def zero_fill(wait):
            def per_e(e, _):
                _run_copies(pad_ref[n_exp + e], zbuf, 0, xs_ref, pad_ref[e], sems.at[2], blk // 2, wait)
                return 0
            lax.fori_loop(0, n_exp, per_e, 0)

            def per_blk(bi, _):
                cp = pltpu.make_async_copy(zbuf, xs_ref.at[pl.ds(pl.multiple_of(bi * blk, blk), blk)],
                                           sems.at[2])
                if wait:
                    cp.wait()
                else:
                    cp.start()
                return 0
            lax.fori_loop(pad_ref[2 * n_exp], n_blocks, per_blk, 0)
        zero_fill(False)
        zero_fill(True)


def _dispatch(pad, cnt_t, ls_t, dst_t, h, p_out, n_rows, n_exp):
    t, d = h.shape
    tm = TM_MERGE
    meta = lambda: pl.BlockSpec((1, 1, LANES), lambda i, pd: (i, 0, 0), memory_space=pltpu.SMEM)
    grid_spec = pltpu.PrefetchScalarGridSpec(
        num_scalar_prefetch=1,
        grid=(t // tm,),
        in_specs=[meta(), meta(), meta(),
                  pl.BlockSpec((tm, d), lambda i, pd: (i, 0)),
                  pl.BlockSpec((tm, LANES), lambda i, pd: (i, 0))],
        out_specs=pl.BlockSpec(memory_space=pl.ANY),
        scratch_shapes=[pltpu.VMEM((2, _stage_rows(tm, n_exp), d // 2), jnp.uint32),
                        pltpu.VMEM((EXPERT_BLOCK, d // 2), jnp.uint32),
                        pltpu.SMEM((2,), I32),
                        pltpu.SemaphoreType.DMA((3,))],
    )
    return pl.pallas_call(
        functools.partial(_dispatch_kernel, n_exp),
        out_shape=jax.ShapeDtypeStruct((n_rows, d // 2), jnp.uint32),
        grid_spec=grid_spec,
        compiler_params=_params(("arbitrary",)),
        name="dispatch",
    )(pad, cnt_t, ls_t, dst_t, h, p_out)


def _experts_kernel(be_ref, used_ref, xs_ref, wg_ref, wu_ref, wd_ref, ys_ref):
    del be_ref

    @pl.when(pl.program_id(0) >= used_ref[0])
    def _():
        ys_ref[...] = jnp.zeros_like(ys_ref)

    @pl.when(pl.program_id(0) < used_ref[0])
    def _():
        half = xs_ref.shape[1]
        lo, hi = _unpack(xs_ref[...])
        gate = _dot(lo, wg_ref[0, :half, :]) + _dot(hi, wg_ref[0, half:, :])
        up = _dot(lo, wu_ref[0, :half, :]) + _dot(hi, wu_ref[0, half:, :])
        ys_ref[...] = _pack(_dot((_silu(gate) * up).astype(BF16), wd_ref[0]))


def _experts(blk_expert, n_used, xs, wg, wu, wd):
    n_rows, half = xs.shape
    blk = EXPERT_BLOCK
    d, de = wg.shape[1], wg.shape[2]
    row = lambda i, be, used: (jnp.minimum(i, used[0] - 1), 0)
    wsel = lambda i, be, used: (be[jnp.minimum(i, used[0] - 1)], 0, 0)
    grid_spec = pltpu.PrefetchScalarGridSpec(
        num_scalar_prefetch=2,
        grid=(n_rows // blk,),
        in_specs=[pl.BlockSpec((blk, half), row),
                  pl.BlockSpec((1, d, de), wsel),
                  pl.BlockSpec((1, d, de), wsel),
                  pl.BlockSpec((1, de, d), wsel)],
        out_specs=pl.BlockSpec((blk, half), lambda i, be, used: (i, 0)),
    )
    return pl.pallas_call(
        _experts_kernel,
        out_shape=jax.ShapeDtypeStruct((n_rows, half), jnp.uint32),
        grid_spec=grid_spec,
        compiler_params=_params(("arbitrary",)),
        name="experts",
    )(blk_expert, n_used, xs, wg, wu, wd)


def _combine_kernel(n_exp, cnt_ref, ls_ref, src_ref, cnt1_ref, ls1_ref, src1_ref, base_ref, p_ref,
                    w_ref, mod_ref, fg_ref, ys_ref, o_ref, stage, acc, sems):
    i = pl.program_id(0)
    n = pl.num_programs(0)
    tm = base_ref.shape[0]
    rows = stage.shape[1]
    half = stage.shape[2]
    top = 1 << (rows.bit_length() - 1)
    slot = i % 2
    total = ls_ref[0, 0, n_exp]

    def gather(c_ref, l_ref, s_ref, sl):
        def per_expert(e, _):
            _run_copies(c_ref[0, 0, e], ys_ref, s_ref[0, 0, e], stage.at[sl], l_ref[0, 0, e],
                        sems.at[sl], tm)
            return 0
        lax.fori_loop(0, n_exp, per_expert, 0)

    @pl.when(i == 0)
    def _():
        stage[...] = jnp.zeros_like(stage)
        gather(cnt_ref, ls_ref, src_ref, 0)

    @pl.when(i + 1 < n)
    def _():
        gather(cnt1_ref, ls1_ref, src1_ref, 1 - slot)

    _run_copies(total, ys_ref, 0, stage.at[slot], 0, sems.at[slot], top, wait=True)

    pt = p_ref[...].astype(F32).T
    wt = w_ref[...].T

    def add_rows(c):
        r = (lax.broadcasted_iota(I32, (tm, tm), 0) + c * tm).astype(F32)
        sel = jnp.zeros((tm, tm), F32)
        for kk in range(TOP_K):
            sel = jnp.where(r == pt[kk:kk + 1, :], wt[kk:kk + 1, :], sel)
        wb = sel.astype(BF16)
        lo, hi = _unpack(stage[slot, c * tm:(c + 1) * tm, :])
        tdot = lambda x: lax.dot_general(wb, x, (((0,), (0,)), ((), ())), preferred_element_type=F32)
        if c == 0:
            acc[:, :half] = tdot(lo)
            acc[:, half:] = tdot(hi)
        else:
            acc[:, :half] += tdot(lo)
            acc[:, half:] += tdot(hi)

    for c in range(rows // tm):
        if c < TOP_K:
            add_rows(c)
        else:
            pl.when(c * tm < total)(functools.partial(add_rows, c))
    y = base_ref[...] + mod_ref[0, 5:6, :] * acc[...]
    o_ref[...] = _rms(y) * fg_ref[...]


def _combine(cnt_t, ls_t, src_t, base, p_out, w8, mod, final_g, ys, seq, n_exp):
    t, d = base.shape
    tm = TM_MERGE
    steps = t // tm
    per_seq = seq // tm
    n6 = mod.shape[1]
    cur = lambda: pl.BlockSpec((1, 1, LANES), lambda i: (i, 0, 0), memory_space=pltpu.SMEM)
    nxt = lambda: pl.BlockSpec((1, 1, LANES), lambda i: (jnp.minimum(i + 1, steps - 1), 0, 0),
                               memory_space=pltpu.SMEM)
    return pl.pallas_call(
        functools.partial(_combine_kernel, n_exp),
        out_shape=jax.ShapeDtypeStruct((t, d), F32),
        grid=(steps,),
        in_specs=[cur(), cur(), cur(), nxt(), nxt(), nxt(),
                  pl.BlockSpec((tm, d), lambda i: (i, 0)),
                  pl.BlockSpec((tm, LANES), lambda i: (i, 0)),
                  pl.BlockSpec((tm, LANES), lambda i: (i, 0)),
                  pl.BlockSpec((1, n6, d), lambda i: (i // per_seq, 0, 0)),
                  pl.BlockSpec((1, d), lambda i: (0, 0)),
                  pl.BlockSpec(memory_space=pl.ANY)],
        out_specs=pl.BlockSpec((tm, d), lambda i: (i, 0)),
        scratch_shapes=[pltpu.VMEM((2, _stage_rows(tm, n_exp), d // 2), jnp.uint32),
                        pltpu.VMEM((tm, d), F32),
                        pltpu.SemaphoreType.DMA((2,))],
        compiler_params=_params(("arbitrary",)),
        name="combine",
    )(cnt_t, ls_t, src_t, cnt_t, ls_t, src_t, base, p_out, w8, mod, final_g.reshape(1, d), ys)


def _prepare(l, ada_w, ada_b, norm1_g, w_in, conv_w, conv_b, lru_wa, lru_ba, lru_wi, lru_bi,
             lru_lambda, gla_w2, gla_b2, gla_norm_g, proj_lru, proj_gla, merge_b, w_out, norm2_g,
             router_w, router_bias, exp_w_gate, exp_w_up, exp_w_down, sh_w_gate, sh_w_up,
             sh_w_down):
    d = w_in.shape[1]
    d_rnn = conv_w.shape[2]
    dkt = gla_w2.shape[3]
    dvt = gla_norm_g.shape[1]
    rank = gla_w2.shape[2]
    sizes = (d_rnn, d_rnn, dkt, dkt, dvt, dvt, 2 * rank, d, d)
    wb = w_in[l].astype(BF16)
    parts, off = [], 0
    for sz in sizes:
        parts.append(wb[:, off:off + sz])
        off += sz
    w2 = gla_w2[l].astype(BF16)
    zeros = jnp.zeros_like(w2[0])
    w2p = jnp.stack([jnp.concatenate([w2[0], zeros], axis=0),
                     jnp.concatenate([zeros, w2[1]], axis=0)])
    return dict(
        ada_w=ada_w[l], ada_b=ada_b[l], norm1_g=norm1_g[l], w_parts=parts,
        conv_w=conv_w[l], conv_b=conv_b[l], wa=lru_wa[l].astype(BF16), wi=lru_wi[l].astype(BF16),
        ba=lru_ba[l], bi=lru_bi[l], lam=lru_lambda[l], w2p=w2p, b2=gla_b2[l],
        gnorm=gla_norm_g[l], proj_lru=proj_lru[l].astype(BF16), proj_gla=proj_gla[l].astype(BF16),
        merge_b=merge_b[l], w_out=w_out[l].astype(BF16), norm2_g=norm2_g[l],
        router_w=router_w[l], router_bias=router_bias[l],
        wg=exp_w_gate[l].astype(BF16), wu=exp_w_up[l].astype(BF16), wd=exp_w_down[l].astype(BF16),
        sg=sh_w_gate[l].astype(BF16), su=sh_w_up[l].astype(BF16), sd=sh_w_down[l].astype(BF16))


def _layer(x, c, p, final_g):
    bsz, s, d = x.shape
    t = bsz * s
    x2 = x.reshape(t, d)
    mod = _ada(c, p["ada_w"], p["ada_b"])
    xr, yr, q, k, v, r, lr, gl, gg = _inproj(x2, mod, p["norm1_g"], p["w_parts"], s)
    seqv = lambda a: a.reshape(bsz, s, a.shape[1])
    y_lru = _lru(seqv(xr), seqv(yr), p["conv_w"], p["conv_b"], p["wa"], p["wi"], p["ba"], p["bi"],
                 p["lam"]).reshape(t, -1)
    y_gla = _gla(seqv(q), seqv(k), seqv(v), seqv(r), seqv(lr), p["w2p"], p["b2"],
                 p["gnorm"]).reshape(t, -1)
    base, h, p_out, w8, cnt_t, ls_t, rb_t, counts = _merge(
        x2, y_lru, y_gla, gl, gg, mod, p["norm2_g"], p["merge_b"], p["proj_lru"], p["proj_gla"],
        p["w_out"], p["router_w"], p["router_bias"], p["sg"], p["su"], p["sd"], s)
    n_exp = p["router_w"].shape[1]
    blk = EXPERT_BLOCK
    n_blocks = -(-(t // TM_MERGE) * _stage_rows(TM_MERGE, n_exp) // blk) + n_exp
    cnt = counts[0, :n_exp]
    padded = (cnt + blk - 1) // blk * blk
    pend = jnp.cumsum(padded)
    pstart = (pend - padded).astype(I32)
    blk_expert = jnp.minimum(
        jnp.sum((pend[None, :] <= (jnp.arange(n_blocks, dtype=I32) * blk)[:, None]).astype(I32), axis=1),
        n_exp - 1).astype(I32)
    row_t = rb_t + jnp.pad(pstart, (0, LANES - n_exp))[None, None, :]
    n_used = (pend[-1:] // blk).astype(I32)
    pad = jnp.concatenate([pstart + cnt, padded - cnt, n_used]).astype(I32)
    xs = _dispatch(pad, cnt_t, ls_t, row_t, h, p_out, n_blocks * blk, n_exp)
    ys = _experts(blk_expert, n_used, xs, p["wg"], p["wu"], p["wd"])
    return _combine(cnt_t, ls_t, row_t, base, p_out, w8, mod, final_g, ys, s,
                    n_exp).reshape(bsz, s, d)


def kernel(x_prompt, x_sample, c_prompt, c_sample, ada_w, ada_b, norm1_g, w_in, conv_w, conv_b,
           lru_wa, lru_ba, lru_wi, lru_bi, lru_lambda, gla_w2, gla_b2, gla_norm_g, proj_lru,
           proj_gla, merge_b, w_out, norm2_g, router_w, router_bias, exp_w_gate, exp_w_up,
           exp_w_down, sh_w_gate, sh_w_up, sh_w_down, final_g):
    depth = ada_w.shape[0]
    assert depth == 1, "the fused final RMSNorm assumes a single layer"
    p = _prepare(0, ada_w, ada_b, norm1_g, w_in, conv_w, conv_b, lru_wa, lru_ba, lru_wi, lru_bi,
                 lru_lambda, gla_w2, gla_b2, gla_norm_g, proj_lru, proj_gla, merge_b, w_out,
                 norm2_g, router_w, router_bias, exp_w_gate, exp_w_up, exp_w_down, sh_w_gate,
                 sh_w_up, sh_w_down)
    return (_layer(x_prompt, c_prompt, p, final_g), _layer(x_sample, c_sample, p, final_g))
```

```python
import functools

import jax
import jax.numpy as jnp
from jax import lax
from jax.experimental import pallas as pl
from jax.experimental.pallas import tpu as pltpu

F32 = jnp.float32
BF16 = jnp.bfloat16
I32 = jnp.int32

EPS = 1e-6
LOG2_E = 1.4426950408889634
LRU_C = 8.0
GLA_HEADS = 4
GLA_TAU = 16.0
GLA_CHUNK = 64
N_GROUPS = 8
TOPK_GROUPS = 4
TOP_K = 8
ROUTED_SCALE = 2.5

LANES = 128
SUBLANES = 8
VMEM_LIMIT = 56 * 1024 * 1024

TM_INPROJ = 512
TM_MERGE = 256
EXPERT_BLOCK_MIN = 256
EXPERT_BLOCK_MAX = 1024
LRU_CHUNK = 256
CONV_PAD = 16


def _dot(a, b):
    return jnp.dot(a, b, preferred_element_type=F32)


def _split_bf16(a):
    hi = a.astype(BF16)
    lo = (a - hi.astype(F32)).astype(BF16)
    return hi, lo


def _dot3(a, b):
    a_hi, a_lo = _split_bf16(a)
    b_hi, b_lo = _split_bf16(b)
    return _dot(a_hi, b_hi) + _dot(a_lo, b_hi) + _dot(a_hi, b_lo)


def _sigmoid(x):
    return 1.0 / (1.0 + jnp.exp(-x))


def _silu(x):
    return x * _sigmoid(x)


def _softplus(z):
    return jnp.maximum(z, 0.0) + jnp.log(1.0 + jnp.exp(-jnp.abs(z)))


def _gelu_tanh(x):
    return 0.5 * x * (1.0 + jnp.tanh(0.7978845608028654 * (x + 0.044715 * (x * x * x))))


def _rms(x):
    return x * lax.rsqrt(jnp.mean(x * x, axis=-1, keepdims=True) + EPS)


def _const_spec(shape):
    zeros = (0,) * len(shape)
    return pl.BlockSpec(shape, lambda *_: zeros, pipeline_mode=pl.Buffered(1))


def _params(sem, vmem=VMEM_LIMIT):
    return pltpu.CompilerParams(dimension_semantics=sem, vmem_limit_bytes=vmem)


def _ada_kernel(c_ref, w_ref, b_ref, o_ref):
    o_ref[...] = _dot3(_silu(c_ref[...]), w_ref[...]) + b_ref[...]


def _ada(c, ada_w, ada_b):
    nb, d = c.shape
    nbp = -(-nb // SUBLANES) * SUBLANES
    cp = jnp.pad(c, ((0, nbp - nb), (0, 0)))
    n6 = ada_w.shape[1] // d
    out = pl.pallas_call(
        _ada_kernel,
        out_shape=jax.ShapeDtypeStruct((nbp, n6 * d), F32),
        grid=(n6,),
        in_specs=[pl.BlockSpec((nbp, d), lambda j: (0, 0)),
                  pl.BlockSpec((d, d), lambda j: (0, j)),
                  pl.BlockSpec((1, d), lambda j: (0, j))],
        out_specs=pl.BlockSpec((nbp, d), lambda j: (0, j)),
        compiler_params=_params(("arbitrary",)),
        name="ada",
    )(cp, ada_w, ada_b.reshape(1, -1))
    return out.reshape(nbp, n6, d)


def _inproj_kernel(x_ref, mod_ref, g_ref, *refs):
    n = len(refs) // 2
    w_refs, o_refs = refs[:n], refs[n:]
    h = _rms(x_ref[...]) * g_ref[...]
    h = h * (1.0 + mod_ref[0, 1:2, :]) + mod_ref[0, 0:1, :]
    hb = h.astype(BF16)
    for w_ref, o_ref in zip(w_refs, o_refs):
        o_ref[...] = _dot(hb, w_ref[...]).astype(o_ref.dtype)


def _inproj(x2, mod, norm_g, w_parts, seq):
    t, d = x2.shape
    tm = TM_INPROJ
    per_seq = seq // tm
    n6 = mod.shape[1]
    in_specs = [pl.BlockSpec((tm, d), lambda i: (i, 0)),
                pl.BlockSpec((1, n6, d), lambda i: (i // per_seq, 0, 0)),
                _const_spec((1, d))]
    in_specs += [_const_spec(w.shape) for w in w_parts]
    out_shape = [jax.ShapeDtypeStruct((t, w.shape[1]), BF16) for w in w_parts]
    out_specs = [pl.BlockSpec((tm, w.shape[1]), lambda i: (i, 0)) for w in w_parts]
    return pl.pallas_call(
        _inproj_kernel,
        out_shape=out_shape,
        grid=(t // tm,),
        in_specs=in_specs,
        out_specs=out_specs,
        compiler_params=_params(("arbitrary",)),
        name="inproj",
    )(x2, mod, norm_g.reshape(1, d), *w_parts)


def _scan_chunk(a, u, h, reverse, store):
    groups = a.shape[0] // SUBLANES
    a = a.reshape(groups, SUBLANES, LANES)
    u = u.reshape(groups, SUBLANES, LANES)
    rm = lax.broadcasted_iota(I32, a.shape, 1)
    for d in (1, 2, 4):
        shift = SUBLANES - d if reverse else d
        keep = (rm < SUBLANES - d) if reverse else (rm >= d)
        u = a * jnp.where(keep, pltpu.roll(u, shift, 1), 0.0) + u
        a = a * jnp.where(keep, pltpu.roll(a, shift, 1), 1.0)
    order = range(groups - 1, -1, -1) if reverse else range(groups)
    for g in order:
        hg = u[g] + a[g] * h
        store(g, hg)
        h = hg[0:1] if reverse else hg[SUBLANES - 1:SUBLANES]
    return h


def _lru_kernel(xr_ref, yr_ref, cw_ref, cb_ref, wa_ref, wi_ref, ba_ref, bi_ref, lam_ref,
                o_ref, xpad, xc, hf, hb):
    s = xr_ref.shape[1]
    ch = LRU_CHUNK
    n = s // ch
    taps = cw_ref.shape[0]
    left = taps // 2
    zpad = jnp.zeros((CONV_PAD, LANES), F32)
    xpad[0:CONV_PAD, :] = zpad
    xpad[CONV_PAD + s:CONV_PAD + s + CONV_PAD, :] = zpad

    def fill(c, _):
        r0 = pl.multiple_of(c * ch, ch)
        xpad[pl.ds(CONV_PAD + r0, ch), :] = xr_ref[0, pl.ds(r0, ch), :].astype(F32)
        return 0
    lax.fori_loop(0, n, fill, 0)

    def conv(c, _):
        r0 = pl.multiple_of(c * ch, ch)
        win = xpad[pl.ds(r0, ch + 2 * CONV_PAD), :]
        acc = jnp.zeros_like(win) + cb_ref[...]
        for i in range(taps):
            sh = (left - i) % (ch + 2 * CONV_PAD)
            src = win if sh == 0 else pltpu.roll(win, sh, 0)
            acc = acc + src * cw_ref[i:i + 1, :]
        xc[pl.ds(r0, ch), :] = acc[CONV_PAD:CONV_PAD + ch]
        return 0
    lax.fori_loop(0, n, conv, 0)

    def gates(r0, d):
        x = xc[pl.ds(r0, ch), :]
        xb = x.astype(BF16)
        r = _sigmoid(_dot(xb, wa_ref[d, 0]) + ba_ref[d:d + 1, :])
        i = _sigmoid(_dot(xb, wi_ref[d, 0]) + bi_ref[d:d + 1, :])
        a = jnp.exp2((-LRU_C * LOG2_E * _softplus(-lam_ref[d:d + 1, :])) * r)
        y = 1.0 - a * a
        u = jnp.where(y > 0.0, y * lax.rsqrt(y), 0.0) * (i * x)
        return a, u

    def step(c, carry):
        h_f, h_b = carry
        rf = pl.multiple_of(c * ch, ch)
        rb = pl.multiple_of((n - 1 - c) * ch, ch)
        a, u = gates(rf, 0)

        def store_f(g, v):
            hf[pl.ds(rf + g * SUBLANES, SUBLANES), :] = v
        h_f = _scan_chunk(a, u, h_f, False, store_f)
        a, u = gates(rb, 1)

        def store_b(g, v):
            hb[pl.ds(rb + g * SUBLANES, SUBLANES), :] = v
        h_b = _scan_chunk(a, u, h_b, True, store_b)
        return h_f, h_b
    zero = jnp.zeros((1, LANES), F32)
    lax.fori_loop(0, n, step, (zero, zero))

    def finish(c, _):
        r0 = pl.multiple_of(c * ch, ch)
        y = yr_ref[0, pl.ds(r0, ch), :].astype(F32)
        o_ref[0, pl.ds(r0, ch), :] = ((hf[pl.ds(r0, ch), :] + hb[pl.ds(r0, ch), :])
                                      * _gelu_tanh(y)).astype(o_ref.dtype)
        return 0
    lax.fori_loop(0, n, finish, 0)


def _lru(xr, yr, conv_w, conv_b, wa, wi, ba, bi, lam):
    b, s, c = xr.shape
    nblk = c // LANES
    seq = lambda: pl.BlockSpec((1, s, LANES), lambda i, j: (i, 0, j))
    chan = lambda rows: pl.BlockSpec((rows, LANES), lambda i, j: (0, j))
    wspec = lambda: pl.BlockSpec((2, 1, LANES, LANES), lambda i, j: (0, j, 0, 0))
    return pl.pallas_call(
        _lru_kernel,
        out_shape=jax.ShapeDtypeStruct((b, s, c), BF16),
        grid=(b, nblk),
        in_specs=[seq(), seq(), chan(conv_w.shape[0]), chan(1), wspec(), wspec(),
                  chan(2), chan(2), chan(2)],
        out_specs=seq(),
        scratch_shapes=[pltpu.VMEM((s + 2 * CONV_PAD, LANES), F32),
                        pltpu.VMEM((s, LANES), F32),
                        pltpu.VMEM((s, LANES), F32),
                        pltpu.VMEM((s, LANES), F32)],
        compiler_params=_params(("arbitrary", "arbitrary")),
        name="lru",
    )(xr, yr, conv_w, conv_b.reshape(1, c), wa, wi, ba, bi, lam)


def _gla_kernel(q_ref, k_ref, v_ref, r_ref, lr_ref, w2_ref, b2_ref, gn_ref, o_ref, acc):
    s = q_ref.shape[1]
    dk = q_ref.shape[2]
    dv = v_ref.shape[2]
    ck = GLA_CHUNK
    n = s // ck
    half = n // 2
    scale = dk ** -0.5
    row = lax.broadcasted_iota(I32, (ck, ck), 0)
    col = lax.broadcasted_iota(I32, (ck, ck), 1)
    tri_f = (row >= col).astype(BF16)
    tri_b = (row <= col).astype(BF16)
    mask_f = row >= col
    mask_b = col > row
    lane2 = lax.broadcasted_iota(I32, (2 * ck, 2 * ck), 1)
    zeros_v = jnp.zeros((ck, dv), BF16)

    def finish(r0, o):
        y = _rms(o) * gn_ref[...]
        rr = r_ref[0, pl.ds(r0, ck), :].astype(F32)
        o_ref[0, pl.ds(r0, ck), :] = (y * _silu(rr)).astype(o_ref.dtype)

    unroll = max(u for u in (1, 2, 4, 8) if half % u == 0)

    def group(it, carry, final):
        states = list(carry)
        jobs = []
        for uu in range(unroll):
            i = it * unroll + uu
            jobs += [(i, 0), (n - 1 - i, 1)]
        rows = [pl.multiple_of(c * ck, ck) for c, _ in jobs]
        pre = [_dot(lr_ref[0, pl.ds(r0, ck), :], w2_ref[d]) + b2_ref[d:d + 1, :]
               for (_, d), r0 in zip(jobs, rows)]
        logb = []
        for (_, d), x in zip(jobs, pre):
            g_hi, g_lo = _split_bf16(-_softplus(-x) * (1.0 / GLA_TAU))
            tri = tri_b if d else tri_f
            logb.append(_dot(tri, g_hi) + _dot(tri, g_lo))
        q_dec, att, k_t, decay = [], [], [], []
        for (_, d), r0, b in zip(jobs, rows, logb):
            q = q_ref[0, pl.ds(r0, ck), :].astype(F32) * scale
            k = k_ref[0, pl.ds(r0, ck), :].astype(F32)
            qd = (q * jnp.exp(b)).astype(BF16)
            kd = (k * jnp.exp(-b)).astype(BF16)
            a = lax.dot_general(qd, kd, (((1,), (1,)), ((), ())), preferred_element_type=F32)
            b_edge = b[0:1] if d else b[ck - 1:ck]
            stacked_t = jnp.concatenate([k * jnp.exp(b_edge - b), b], axis=0).T
            edge_lane = ck if d else 2 * ck - 1
            q_dec.append(qd)
            att.append(jnp.where(mask_b if d else mask_f, a, 0.0).astype(BF16))
            decay.append(jnp.exp(stacked_t[:, edge_lane:edge_lane + 1]))
            k_t.append(jnp.where(lane2 < ck, stacked_t, 0.0).astype(BF16))
        intra, inject = [], []
        for r0, a, kt in zip(rows, att, k_t):
            v = v_ref[0, pl.ds(r0, ck), :]
            intra.append(_dot(a, v))
            inject.append(_dot(kt, jnp.concatenate([v, zeros_v], axis=0)))
        seen = []
        for (_, d), dc, inj in zip(jobs, decay, inject):
            seen.append(states[d].astype(BF16))
            states[d] = states[d] * dc + inj
        for r0, qd, st, o in zip(rows, q_dec, seen, intra):
            o = o + _dot(qd, st)
            if final:
                finish(r0, o + acc[pl.ds(r0, ck), :])
            else:
                acc[pl.ds(r0, ck), :] = o
        return tuple(states)

    zero = jnp.zeros((dk, dv), F32)
    carry = lax.fori_loop(0, half // unroll, functools.partial(group, final=False), (zero, zero))
    lax.fori_loop(half // unroll, n // unroll, functools.partial(group, final=True), carry)


def _gla(q, k, v, r, lr, w2p, b2, gnorm):
    b, s, dkt = q.shape
    dvt = v.shape[2]
    h = GLA_HEADS
    dk, dv = dkt // h, dvt // h
    nlr = lr.shape[2]
    kspec = lambda: pl.BlockSpec((1, s, dk), lambda i, j: (i, 0, j))
    vspec = lambda: pl.BlockSpec((1, s, dv), lambda i, j: (i, 0, j))
    return pl.pallas_call(
        _gla_kernel,
        out_shape=jax.ShapeDtypeStruct((b, s, dvt), BF16),
        grid=(b, h),
        in_specs=[kspec(), kspec(), vspec(), vspec(),
                  pl.BlockSpec((1, s, nlr), lambda i, j: (i, 0, 0)),
                  pl.BlockSpec((2, nlr, dk), lambda i, j: (0, 0, j)),
                  pl.BlockSpec((2, dk), lambda i, j: (0, j)),
                  pl.BlockSpec((1, dv), lambda i, j: (0, j))],
        out_specs=vspec(),
        scratch_shapes=[pltpu.VMEM((s, dv), F32)],
        compiler_params=_params(("arbitrary", "arbitrary")),
        name="gla",
    )(q, k, v, r, lr, w2p, b2, gnorm.reshape(1, dvt))


def _group_reduce(x, lane, op):
    for sft in (1, 2, 4):
        up = pltpu.roll(x, LANES - sft, 1)
        dn = pltpu.roll(x, sft, 1)
        x = op(x, jnp.where((lane & sft) == 0, up, dn))
    return x


def _route(scores, sel, n_exp):
    neg = -jnp.inf
    lane = lax.broadcasted_iota(I32, scores.shape, 1)
    lane_f = lane.astype(F32)
    per_group = n_exp // N_GROUPS
    assert per_group == SUBLANES and n_exp <= LANES
    valid = lane < n_exp
    v = jnp.where(valid, sel, neg)
    m1 = _group_reduce(v, lane, jnp.maximum)
    first = _group_reduce(jnp.where(v == m1, lane_f, float(LANES)), lane, jnp.minimum)
    m2 = _group_reduce(jnp.where(lane_f == first, neg, v), lane, jnp.maximum)
    gs = m1 + m2
    gidx = lane >> 3
    n_slots = LANES // per_group
    rank = jnp.zeros(scores.shape, F32)
    for kk in range(1, n_slots):
        other = pltpu.roll(gs, per_group * kk, 1)
        og = (gidx - kk) & (n_slots - 1)
        better = jnp.where(other > gs, 1.0, jnp.where(other == gs, jnp.where(og < gidx, 1.0, 0.0), 0.0))
        rank = rank + better
    w = jnp.where(rank < float(TOPK_GROUPS), v, neg)
    cols = []
    chosen = jnp.zeros(scores.shape, F32)
    for _ in range(TOP_K):
        m = jnp.max(w, axis=1, keepdims=True)
        j = jnp.min(jnp.where(w == m, lane_f, float(LANES)), axis=1, keepdims=True)
        pick = lane_f == j
        w = jnp.where(pick, neg, w)
        chosen = jnp.where(pick, 1.0, chosen)
        cols.append(j)
    return cols, chosen, lane_f


def _merge_kernel(n_exp, x_ref, yl_ref, yg_ref, gl_ref, gg_ref, mod_ref, n2_ref, mb_ref,
                  pl_ref, pg_ref, wo_ref, rw_ref, rb_ref, sg_ref, su_ref, sd_ref,
                  base_ref, h_ref, p_ref, w_ref, cnt_ref, ls_ref, rb4_ref, tot_ref, run, lg):
    tm = x_ref.shape[0]
    i = pl.program_id(0)

    @pl.when(i == 0)
    def _():
        run[...] = jnp.zeros_like(run)
        lg[...] = jnp.zeros_like(lg)

    prev_logits = lg[...]
    a = _dot(yl_ref[...], pl_ref[...])
    b = _dot(yg_ref[...], pg_ref[...])
    merged = (_sigmoid(gl_ref[...].astype(F32) + mb_ref[0:1, :]) * a
              + _sigmoid(gg_ref[...].astype(F32) + mb_ref[1:2, :]) * b)
    mix = _dot(merged.astype(BF16), wo_ref[...])
    x1 = x_ref[...] + mod_ref[0, 2:3, :] * mix
    h = _rms(x1) * n2_ref[...]
    h = h * (1.0 + mod_ref[0, 4:5, :]) + mod_ref[0, 3:4, :]
    hb = h.astype(BF16)
    h_ref[...] = hb
    shared = _dot(( _silu(_dot(hb, sg_ref[...])) * _dot(hb, su_ref[...]) ).astype(BF16), sd_ref[...])
    base_ref[...] = x1 + mod_ref[0, 5:6, :] * shared

    lg[...] = _dot3(h, rw_ref[...])

    scores = _sigmoid(prev_logits)
    cols, chosen, lane_f = _route(scores, scores + rb_ref[...], n_exp)
    picked = jnp.where(chosen > 0.0, scores, 0.0)
    cw = picked * (ROUTED_SCALE / jnp.sum(picked, axis=1, keepdims=True))
    chosen = chosen * jnp.where(i > 0, 1.0, 0.0)

    r = lax.broadcasted_iota(I32, (tm, tm), 0)
    c = lax.broadcasted_iota(I32, (tm, tm), 1)
    before = (c < r).astype(BF16)
    rank = _dot(before, chosen.astype(BF16))
    cnt = rank[tm - 1:tm, :] + chosen[tm - 1:tm, :]
    cnt8 = jnp.floor((cnt + (SUBLANES - 1)) * (1.0 / SUBLANES)) * SUBLANES
    lr_ = lax.broadcasted_iota(I32, (LANES, LANES), 0)
    lc_ = lax.broadcasted_iota(I32, (LANES, LANES), 1)
    lstart = _dot(jnp.broadcast_to(cnt8, (SUBLANES, LANES)).astype(BF16),
                  (lr_ < lc_).astype(BF16))[0:1]
    pos = rank + lstart
    cnt_ref[0] = cnt8.astype(I32)
    ls_ref[0] = lstart.astype(I32)
    rb4_ref[0] = run[...].astype(I32)
    run[...] = run[...] + cnt8
    tot_ref[...] = run[...].astype(I32)

    p_out = jnp.zeros(scores.shape, F32)
    w_out = jnp.zeros(scores.shape, F32)
    for kk, j in enumerate(cols):
        hit = lane_f == j
        pk = jnp.sum(jnp.where(hit, pos, 0.0), axis=1, keepdims=True)
        wk = jnp.sum(jnp.where(hit, cw, 0.0), axis=1, keepdims=True)
        slot = lane_f == float(kk)
        p_out = jnp.where(slot, pk, p_out)
        w_out = jnp.where(slot, wk, w_out)
    p_ref[...] = p_out.astype(I32)
    w_ref[...] = w_out


def _merge(x2, y_lru, y_gla, gl, gg, mod, norm2_g, merge_b, proj_lru, proj_gla, w_out,
           router_w, router_bias, sh_gate, sh_up, sh_down, seq):
    t, d = x2.shape
    tm = TM_MERGE
    per_seq = seq // tm
    n6 = mod.shape[1]
    n_exp = router_w.shape[1]
    rw = jnp.pad(router_w, ((0, 0), (0, LANES - n_exp)))
    rb = jnp.pad(router_bias.reshape(1, n_exp), ((0, 0), (0, LANES - n_exp)))
    n_tiles = t // tm
    proj = lambda i: jnp.minimum(i, n_tiles - 1)
    routed = lambda i: jnp.maximum(i - 1, 0)
    tok = lambda width: pl.BlockSpec((tm, width), lambda i: (proj(i), 0))
    in_specs = [tok(d), tok(y_lru.shape[1]), tok(y_gla.shape[1]), tok(d), tok(d),
                pl.BlockSpec((1, n6, d), lambda i: (proj(i) // per_seq, 0, 0)),
                _const_spec((1, d)), _const_spec((2, d)),
                _const_spec(proj_lru.shape), _const_spec(proj_gla.shape), _const_spec(w_out.shape),
                _const_spec(rw.shape), _const_spec(rb.shape),
                _const_spec(sh_gate.shape), _const_spec(sh_up.shape), _const_spec(sh_down.shape)]
    meta = jax.ShapeDtypeStruct((n_tiles, 1, LANES), I32)
    meta_spec = lambda: pl.BlockSpec((1, 1, LANES), lambda i: (routed(i), 0, 0))
    plan = lambda: pl.BlockSpec((tm, LANES), lambda i: (routed(i), 0))
    out_shape = [jax.ShapeDtypeStruct((t, d), F32), jax.ShapeDtypeStruct((t, d), BF16),
                 jax.ShapeDtypeStruct((t, LANES), I32), jax.ShapeDtypeStruct((t, LANES), F32),
                 meta, meta, meta, jax.ShapeDtypeStruct((1, LANES), I32)]
    out_specs = [tok(d), tok(d), plan(), plan(), meta_spec(), meta_spec(), meta_spec(),
                 pl.BlockSpec((1, LANES), lambda i: (0, 0))]
    return pl.pallas_call(
        functools.partial(_merge_kernel, n_exp),
        out_shape=out_shape,
        grid=(n_tiles + 1,),
        in_specs=in_specs,
        out_specs=out_specs,
        scratch_shapes=[pltpu.VMEM((1, LANES), F32), pltpu.VMEM((tm, LANES), F32)],
        compiler_params=_params(("arbitrary",)),
        name="merge",
    )(x2, y_lru, y_gla, gl, gg, mod, norm2_g.reshape(1, d), merge_b, proj_lru, proj_gla, w_out,
      rw, rb, sh_gate, sh_up, sh_down)


def _run_copies(n, src_ref, src0, dst_ref, dst0, sem, max_rows, wait=False):
    for b in range(max_rows.bit_length() - 1, 2, -1):
        size = 1 << b

        @pl.when((n & size) != 0)
        def _(b=b, size=size):
            off = (n >> (b + 1)) << (b + 1)
            cp = pltpu.make_async_copy(
                src_ref.at[pl.ds(pl.multiple_of(src0 + off, SUBLANES), size)],
                dst_ref.at[pl.ds(pl.multiple_of(dst0 + off, SUBLANES), size)], sem)
            if wait:
                cp.wait()
            else:
                cp.start()


def _row_index_bf16(tm):
    assert tm <= 256
    return lax.broadcasted_iota(I32, (tm, tm), 0).astype(F32).astype(BF16)


def _pack(x):
    half = x.shape[1] // 2
    bits = lax.bitcast_convert_type(x.astype(BF16).astype(F32), jnp.uint32)
    return (bits[:, :half] >> 16) | (bits[:, half:] & jnp.uint32(0xFFFF0000))


def _unpack(w):
    lo = lax.bitcast_convert_type(w << 16, F32)
    hi = lax.bitcast_convert_type(w & jnp.uint32(0xFFFF0000), F32)
    return lo.astype(BF16), hi.astype(BF16)


def _stage_rows(tm, n_exp):
    return -(-(TOP_K * tm + SUBLANES * n_exp) // tm) * tm


def _dispatch_kernel(n_exp, pad_ref, cnt_ref, ls_ref, dst_ref, h_ref, p_ref, xs_ref,
                     stage, zbuf, sent, sems):
    i = pl.program_id(0)
    n = pl.num_programs(0)
    tm = h_ref.shape[0]
    rows = stage.shape[1]
    top = 1 << (rows.bit_length() - 1)
    slot = i % 2
    total = ls_ref[0, 0, n_exp]

    def drain(sl):
        _run_copies(sent[sl], stage.at[sl], 0, xs_ref, 0, sems.at[sl], top, wait=True)

    @pl.when(i >= 2)
    def _():
        drain(slot)

    pt = p_ref[...].astype(F32).T
    r = _row_index_bf16(tm)

    def sort_rows(c):
        sel = jnp.zeros((tm, tm), BF16)
        for kk in range(TOP_K):
            sel = jnp.where(r == (pt[kk:kk + 1, :] - float(c * tm)).astype(BF16), 1.0, sel)
        stage[slot, c * tm:(c + 1) * tm, :] = _pack(_dot(sel, h_ref[...]))

    for c in range(rows // tm):
        if c < TOP_K:
            sort_rows(c)
        else:
            pl.when(c * tm < total)(functools.partial(sort_rows, c))

    def per_expert(e, _):
        _run_copies(cnt_ref[0, 0, e], stage.at[slot], ls_ref[0, 0, e], xs_ref, dst_ref[0, 0, e],
                    sems.at[slot], tm)
        return 0
    lax.fori_loop(0, n_exp, per_expert, 0)
    sent[slot] = total

    @pl.when(i == n - 1)
    def _():
        drain(slot)

        @pl.when(n >= 2)
        def _():
            drain(1 - slot)
        blk = zbuf.shape[0]
        zbuf[...] = jnp.zeros_like(zbuf)
        n_blocks = xs_ref.shape[0] // blk

        def zero_fill(wait):
            def per_e(e, _):
                _run_copies(pad_ref[n_exp + e], zbuf, 0, xs_ref, pad_ref[e], sems.at[2], blk // 2, wait)
                return 0
            lax.fori_loop(0, n_exp, per_e, 0)

            def per_blk(bi, _):
                cp = pltpu.make_async_copy(zbuf, xs_ref.at[pl.ds(pl.multiple_of(bi * blk, blk), blk)],
                                           sems.at[2])
                if wait:
                    cp.wait()
                else:
                    cp.start()
                return 0
            lax.fori_loop(pad_ref[2 * n_exp], n_blocks, per_blk, 0)
        zero_fill(False)
        zero_fill(True)


def _expert_block(t, n_exp):
    mean_rows = t * TOP_K // n_exp
    blk = EXPERT_BLOCK_MIN
    while blk < EXPERT_BLOCK_MAX and blk * 8 <= mean_rows:
        blk *= 2
    return blk


def _dispatch(pad, cnt_t, ls_t, dst_t, h, p_out, n_rows, n_exp, blk):
    t, d = h.shape
    tm = TM_MERGE
    meta = lambda: pl.BlockSpec((1, 1, LANES), lambda i, pd: (i, 0, 0), memory_space=pltpu.SMEM)
    grid_spec = pltpu.PrefetchScalarGridSpec(
        num_scalar_prefetch=1,
        grid=(t // tm,),
        in_specs=[meta(), meta(), meta(),
                  pl.BlockSpec((tm, d), lambda i, pd: (i, 0)),
                  pl.BlockSpec((tm, LANES), lambda i, pd: (i, 0))],
        out_specs=pl.BlockSpec(memory_space=pl.ANY),
        scratch_shapes=[pltpu.VMEM((2, _stage_rows(tm, n_exp), d // 2), jnp.uint32),
                        pltpu.VMEM((blk, d // 2), jnp.uint32),
                        pltpu.SMEM((2,), I32),
                        pltpu.SemaphoreType.DMA((3,))],
    )
    return pl.pallas_call(
        functools.partial(_dispatch_kernel, n_exp),
        out_shape=jax.ShapeDtypeStruct((n_rows, d // 2), jnp.uint32),
        grid_spec=grid_spec,
        compiler_params=_params(("arbitrary",)),
        name="dispatch",
    )(pad, cnt_t, ls_t, dst_t, h, p_out)


def _experts_kernel(be_ref, used_ref, xs_ref, wg_ref, wu_ref, wd_ref, ys_ref):
    del be_ref

    @pl.when(pl.program_id(0) >= used_ref[0])
    def _():
        ys_ref[...] = jnp.zeros_like(ys_ref)

    @pl.when(pl.program_id(0) < used_ref[0])
    def _():
        half = xs_ref.shape[1]
        lo, hi = _unpack(xs_ref[...])
        gate = _dot(lo, wg_ref[0, :half, :]) + _dot(hi, wg_ref[0, half:, :])
        up = _dot(lo, wu_ref[0, :half, :]) + _dot(hi, wu_ref[0, half:, :])
        ys_ref[...] = _pack(_dot((_silu(gate) * up).astype(BF16), wd_ref[0]))


def _experts(blk_expert, n_used, xs, wg, wu, wd, blk):
    n_rows, half = xs.shape
    d, de = wg.shape[1], wg.shape[2]
    row = lambda i, be, used: (jnp.minimum(i, used[0] - 1), 0)
    wsel = lambda i, be, used: (be[jnp.minimum(i, used[0] - 1)], 0, 0)
    grid_spec = pltpu.PrefetchScalarGridSpec(
        num_scalar_prefetch=2,
        grid=(n_rows // blk,),
        in_specs=[pl.BlockSpec((blk, half), row),
                  pl.BlockSpec((1, d, de), wsel),
                  pl.BlockSpec((1, d, de), wsel),
                  pl.BlockSpec((1, de, d), wsel)],
        out_specs=pl.BlockSpec((blk, half), lambda i, be, used: (i, 0)),
    )
    return pl.pallas_call(
        _experts_kernel,
        out_shape=jax.ShapeDtypeStruct((n_rows, half), jnp.uint32),
        grid_spec=grid_spec,
        compiler_params=_params(("arbitrary",)),
        name="experts",
    )(blk_expert, n_used, xs, wg, wu, wd)


def _combine_kernel(n_exp, cnt_ref, ls_ref, src_ref, cnt1_ref, ls1_ref, src1_ref, base_ref, p_ref,
                    w_ref, mod_ref, fg_ref, ys_ref, o_ref, stage, acc, sems):
    i = pl.program_id(0)
    n = pl.num_programs(0)
    tm = base_ref.shape[0]
    rows = stage.shape[1]
    half = stage.shape[2]
    top = 1 << (rows.bit_length() - 1)
    slot = i % 2
    total = ls_ref[0, 0, n_exp]

    def gather(c_ref, l_ref, s_ref, sl):
        def per_expert(e, _):
            _run_copies(c_ref[0, 0, e], ys_ref, s_ref[0, 0, e], stage.at[sl], l_ref[0, 0, e],
                        sems.at[sl], tm)
            return 0
        lax.fori_loop(0, n_exp, per_expert, 0)

    @pl.when(i == 0)
    def _():
        stage[...] = jnp.zeros_like(stage)
        gather(cnt_ref, ls_ref, src_ref, 0)

    @pl.when(i + 1 < n)
    def _():
        gather(cnt1_ref, ls1_ref, src1_ref, 1 - slot)

    _run_copies(total, ys_ref, 0, stage.at[slot], 0, sems.at[slot], top, wait=True)

    pt = p_ref[...].astype(F32).T
    wt = w_ref[...].T

    r = _row_index_bf16(tm)

    def add_rows(c):
        wb = jnp.zeros((tm, tm), BF16)
        for kk in range(TOP_K):
            wb = jnp.where(r == (pt[kk:kk + 1, :] - float(c * tm)).astype(BF16),
                           wt[kk:kk + 1, :].astype(BF16), wb)
        lo, hi = _unpack(stage[slot, c * tm:(c + 1) * tm, :])
        tdot = lambda x: lax.dot_general(wb, x, (((0,), (0,)), ((), ())), preferred_element_type=F32)
        if c == 0:
            acc[:, :half] = tdot(lo)
            acc[:, half:] = tdot(hi)
        else:
            acc[:, :half] += tdot(lo)
            acc[:, half:] += tdot(hi)

    for c in range(rows // tm):
        if c < TOP_K:
            add_rows(c)
        else:
            pl.when(c * tm < total)(functools.partial(add_rows, c))
    y = base_ref[...] + mod_ref[0, 5:6, :] * acc[...]
    o_ref[...] = _rms(y) * fg_ref[...]


def _combine(cnt_t, ls_t, src_t, base, p_out, w8, mod, final_g, ys, seq, n_exp):
    t, d = base.shape
    tm = TM_MERGE
    steps = t // tm
    per_seq = seq // tm
    n6 = mod.shape[1]
    cur = lambda: pl.BlockSpec((1, 1, LANES), lambda i: (i, 0, 0), memory_space=pltpu.SMEM)
    nxt = lambda: pl.BlockSpec((1, 1, LANES), lambda i: (jnp.minimum(i + 1, steps - 1), 0, 0),
                               memory_space=pltpu.SMEM)
    return pl.pallas_call(
        functools.partial(_combine_kernel, n_exp),
        out_shape=jax.ShapeDtypeStruct((t, d), F32),
        grid=(steps,),
        in_specs=[cur(), cur(), cur(), nxt(), nxt(), nxt(),
                  pl.BlockSpec((tm, d), lambda i: (i, 0)),
                  pl.BlockSpec((tm, LANES), lambda i: (i, 0)),
                  pl.BlockSpec((tm, LANES), lambda i: (i, 0)),
                  pl.BlockSpec((1, n6, d), lambda i: (i // per_seq, 0, 0)),
                  pl.BlockSpec((1, d), lambda i: (0, 0)),
                  pl.BlockSpec(memory_space=pl.ANY)],
        out_specs=pl.BlockSpec((tm, d), lambda i: (i, 0)),
        scratch_shapes=[pltpu.VMEM((2, _stage_rows(tm, n_exp), d // 2), jnp.uint32),
                        pltpu.VMEM((tm, d), F32),
                        pltpu.SemaphoreType.DMA((2,))],
        compiler_params=_params(("arbitrary",)),
        name="combine",
    )(cnt_t, ls_t, src_t, cnt_t, ls_t, src_t, base, p_out, w8, mod, final_g.reshape(1, d), ys)


def _prepare(l, ada_w, ada_b, norm1_g, w_in, conv_w, conv_b, lru_wa, lru_ba, lru_wi, lru_bi,
             lru_lambda, gla_w2, gla_b2, gla_norm_g, proj_lru, proj_gla, merge_b, w_out, norm2_g,
             router_w, router_bias, exp_w_gate, exp_w_up, exp_w_down, sh_w_gate, sh_w_up,
             sh_w_down):
    d = w_in.shape[1]
    d_rnn = conv_w.shape[2]
    dkt = gla_w2.shape[3]
    dvt = gla_norm_g.shape[1]
    rank = gla_w2.shape[2]
    sizes = (d_rnn, d_rnn, dkt, dkt, dvt, dvt, 2 * rank, d, d)
    wb = w_in[l].astype(BF16)
    parts, off = [], 0
    for sz in sizes:
        parts.append(wb[:, off:off + sz])
        off += sz
    w2 = gla_w2[l].astype(BF16)
    zeros = jnp.zeros_like(w2[0])
    w2p = jnp.stack([jnp.concatenate([w2[0], zeros], axis=0),
                     jnp.concatenate([zeros, w2[1]], axis=0)])
    return dict(
        ada_w=ada_w[l], ada_b=ada_b[l], norm1_g=norm1_g[l], w_parts=parts,
        conv_w=conv_w[l], conv_b=conv_b[l], wa=lru_wa[l].astype(BF16), wi=lru_wi[l].astype(BF16),
        ba=lru_ba[l], bi=lru_bi[l], lam=lru_lambda[l], w2p=w2p, b2=gla_b2[l],
        gnorm=gla_norm_g[l], proj_lru=proj_lru[l].astype(BF16), proj_gla=proj_gla[l].astype(BF16),
        merge_b=merge_b[l], w_out=w_out[l].astype(BF16), norm2_g=norm2_g[l],
        router_w=router_w[l], router_bias=router_bias[l],
        wg=exp_w_gate[l].astype(BF16), wu=exp_w_up[l].astype(BF16), wd=exp_w_down[l].astype(BF16),
        sg=sh_w_gate[l].astype(BF16), su=sh_w_up[l].astype(BF16), sd=sh_w_down[l].astype(BF16))


def _layer(x, c, p, final_g):
    bsz, s, d = x.shape
    t = bsz * s
    x2 = x.reshape(t, d)
    mod = _ada(c, p["ada_w"], p["ada_b"])
    xr, yr, q, k, v, r, lr, gl, gg = _inproj(x2, mod, p["norm1_g"], p["w_parts"], s)
    seqv = lambda a: a.reshape(bsz, s, a.shape[1])
    y_lru = _lru(seqv(xr), seqv(yr), p["conv_w"], p["conv_b"], p["wa"], p["wi"], p["ba"], p["bi"],
                 p["lam"]).reshape(t, -1)
    y_gla = _gla(seqv(q), seqv(k), seqv(v), seqv(r), seqv(lr), p["w2p"], p["b2"],
                 p["gnorm"]).reshape(t, -1)
    base, h, p_out, w8, cnt_t, ls_t, rb_t, counts = _merge(
        x2, y_lru, y_gla, gl, gg, mod, p["norm2_g"], p["merge_b"], p["proj_lru"], p["proj_gla"],
        p["w_out"], p["router_w"], p["router_bias"], p["sg"], p["su"], p["sd"], s)
    n_exp = p["router_w"].shape[1]
    blk = _expert_block(t, n_exp)
    n_blocks = -(-(t // TM_MERGE) * _stage_rows(TM_MERGE, n_exp) // blk) + n_exp
    cnt = counts[0, :n_exp]
    padded = (cnt + blk - 1) // blk * blk
    pend = jnp.cumsum(padded)
    pstart = (pend - padded).astype(I32)
    blk_expert = jnp.minimum(
        jnp.sum((pend[None, :] <= (jnp.arange(n_blocks, dtype=I32) * blk)[:, None]).astype(I32), axis=1),
        n_exp - 1).astype(I32)
    row_t = rb_t + jnp.pad(pstart, (0, LANES - n_exp))[None, None, :]
    n_used = (pend[-1:] // blk).astype(I32)
    pad = jnp.concatenate([pstart + cnt, padded - cnt, n_used]).astype(I32)
    xs = _dispatch(pad, cnt_t, ls_t, row_t, h, p_out, n_blocks * blk, n_exp, blk)
    ys = _experts(blk_expert, n_used, xs, p["wg"], p["wu"], p["wd"], blk)
    return _combine(cnt_t, ls_t, row_t, base, p_out, w8, mod, final_g, ys, s,
                    n_exp).reshape(bsz, s, d)


def kernel(x_prompt, x_sample, c_prompt, c_sample, ada_w, ada_b, norm1_g, w_in, conv_w, conv_b,
           lru_wa, lru_ba, lru_wi, lru_bi, lru_lambda, gla_w2, gla_b2, gla_norm_g, proj_lru,
           proj_gla, merge_b, w_out, norm2_g, router_w, router_bias, exp_w_gate, exp_w_up,
           exp_w_down, sh_w_gate, sh_w_up, sh_w_down, final_g):
    depth = ada_w.shape[0]
    assert depth == 1, "the fused final RMSNorm assumes a single layer"
    p = _prepare(0, ada_w, ada_b, norm1_g, w_in, conv_w, conv_b, lru_wa, lru_ba, lru_wi, lru_bi,
                 lru_lambda, gla_w2, gla_b2, gla_norm_g, proj_lru, proj_gla, merge_b, w_out,
                 norm2_g, router_w, router_bias, exp_w_gate, exp_w_up, exp_w_down, sh_w_gate,
                 sh_w_up, sh_w_down)
    return (_layer(x_prompt, c_prompt, p, final_g), _layer(x_sample, c_sample, p, final_g))
```

```python
import functools

import jax
import jax.numpy as jnp
from jax import lax
from jax.experimental import pallas as pl
from jax.experimental.pallas import tpu as pltpu

F32 = jnp.float32
BF16 = jnp.bfloat16
I32 = jnp.int32

EPS = 1e-6
LOG2_E = 1.4426950408889634
LRU_C = 8.0
GLA_HEADS = 4
GLA_TAU = 16.0
GLA_CHUNK = 64
N_GROUPS = 8
TOPK_GROUPS = 4
TOP_K = 8
ROUTED_SCALE = 2.5

LANES = 128
SUBLANES = 8
VMEM_LIMIT = 56 * 1024 * 1024

TM_INPROJ = 512
TM_MERGE = 256
EXPERT_BLOCK_MIN = 256
EXPERT_BLOCK_MAX = 1024
LONG_RUN = 64
LRU_CHUNK = 256
CONV_PAD = 16


def _dot(a, b):
    return jnp.dot(a, b, preferred_element_type=F32)


def _split_bf16(a):
    hi = a.astype(BF16)
    lo = (a - hi.astype(F32)).astype(BF16)
    return hi, lo


def _dot3(a, b):
    a_hi, a_lo = _split_bf16(a)
    b_hi, b_lo = _split_bf16(b)
    return _dot(a_hi, b_hi) + _dot(a_lo, b_hi) + _dot(a_hi, b_lo)


def _sigmoid(x):
    return 1.0 / (1.0 + jnp.exp(-x))


def _silu(x):
    return x * _sigmoid(x)


def _softplus(z):
    return jnp.maximum(z, 0.0) + jnp.log(1.0 + jnp.exp(-jnp.abs(z)))


def _gelu_tanh(x):
    return 0.5 * x * (1.0 + jnp.tanh(0.7978845608028654 * (x + 0.044715 * (x * x * x))))


def _rms(x):
    return x * lax.rsqrt(jnp.mean(x * x, axis=-1, keepdims=True) + EPS)


def _const_spec(shape):
    zeros = (0,) * len(shape)
    return pl.BlockSpec(shape, lambda *_: zeros, pipeline_mode=pl.Buffered(1))


def _params(sem, vmem=VMEM_LIMIT):
    return pltpu.CompilerParams(dimension_semantics=sem, vmem_limit_bytes=vmem)


def _ada_kernel(c_ref, w_ref, b_ref, o_ref):
    o_ref[...] = _dot3(_silu(c_ref[...]), w_ref[...]) + b_ref[...]


def _ada(c, ada_w, ada_b):
    nb, d = c.shape
    nbp = -(-nb // SUBLANES) * SUBLANES
    cp = jnp.pad(c, ((0, nbp - nb), (0, 0)))
    n6 = ada_w.shape[1] // d
    out = pl.pallas_call(
        _ada_kernel,
        out_shape=jax.ShapeDtypeStruct((nbp, n6 * d), F32),
        grid=(n6,),
        in_specs=[pl.BlockSpec((nbp, d), lambda j: (0, 0)),
                  pl.BlockSpec((d, d), lambda j: (0, j)),
                  pl.BlockSpec((1, d), lambda j: (0, j))],
        out_specs=pl.BlockSpec((nbp, d), lambda j: (0, j)),
        compiler_params=_params(("arbitrary",)),
        name="ada",
    )(cp, ada_w, ada_b.reshape(1, -1))
    return out.reshape(nbp, n6, d)


def _inproj_kernel(x_ref, mod_ref, g_ref, *refs):
    n = len(refs) // 2
    w_refs, o_refs = refs[:n], refs[n:]
    h = _rms(x_ref[...]) * g_ref[...]
    h = h * (1.0 + mod_ref[0, 1:2, :]) + mod_ref[0, 0:1, :]
    hb = h.astype(BF16)
    for w_ref, o_ref in zip(w_refs, o_refs):
        o_ref[...] = _dot(hb, w_ref[...]).astype(o_ref.dtype)


def _inproj(x2, mod, norm_g, w_parts, seq):
    t, d = x2.shape
    tm = TM_INPROJ
    per_seq = seq // tm
    n6 = mod.shape[1]
    in_specs = [pl.BlockSpec((tm, d), lambda i: (i, 0)),
                pl.BlockSpec((1, n6, d), lambda i: (i // per_seq, 0, 0)),
                _const_spec((1, d))]
    in_specs += [_const_spec(w.shape) for w in w_parts]
    out_shape = [jax.ShapeDtypeStruct((t, w.shape[1]), BF16) for w in w_parts]
    out_specs = [pl.BlockSpec((tm, w.shape[1]), lambda i: (i, 0)) for w in w_parts]
    return pl.pallas_call(
        _inproj_kernel,
        out_shape=out_shape,
        grid=(t // tm,),
        in_specs=in_specs,
        out_specs=out_specs,
        compiler_params=_params(("arbitrary",)),
        name="inproj",
    )(x2, mod, norm_g.reshape(1, d), *w_parts)


def _scan_chunk(a, u, h, reverse, store):
    groups = a.shape[0] // SUBLANES
    a = a.reshape(groups, SUBLANES, a.shape[1])
    u = u.reshape(groups, SUBLANES, u.shape[1])
    rm = lax.broadcasted_iota(I32, a.shape, 1)
    for d in (1, 2, 4):
        shift = SUBLANES - d if reverse else d
        keep = (rm < SUBLANES - d) if reverse else (rm >= d)
        u = a * jnp.where(keep, pltpu.roll(u, shift, 1), 0.0) + u
        a = a * jnp.where(keep, pltpu.roll(a, shift, 1), 1.0)
    order = range(groups - 1, -1, -1) if reverse else range(groups)
    for g in order:
        hg = u[g] + a[g] * h
        store(g, hg)
        h = hg[0:1] if reverse else hg[SUBLANES - 1:SUBLANES]
    return h


def _lru_kernel(xr_ref, yr_ref, cw_ref, cb_ref, wa_ref, wi_ref, ba_ref, bi_ref, lam_ref,
                o_ref, xpad, xc, acc_ref):
    s = xr_ref.shape[1]
    width = xr_ref.shape[2]
    ch = LRU_CHUNK
    n = s // ch
    taps = cw_ref.shape[0]
    left = taps // 2
    zpad = jnp.zeros((CONV_PAD, width), xpad.dtype)
    xpad[0:CONV_PAD, :] = zpad
    xpad[CONV_PAD + s:CONV_PAD + s + CONV_PAD, :] = zpad

    def fill(c, _):
        r0 = pl.multiple_of(c * ch, ch)
        xpad[pl.ds(CONV_PAD + r0, ch), :] = xr_ref[0, pl.ds(r0, ch), :]
        return 0
    lax.fori_loop(0, n, fill, 0)

    def conv(c, _):
        r0 = pl.multiple_of(c * ch, ch)
        win = xpad[pl.ds(r0, ch + 2 * CONV_PAD), :].astype(F32)
        acc = jnp.zeros_like(win) + cb_ref[...]
        for i in range(taps):
            sh = (left - i) % (ch + 2 * CONV_PAD)
            src = win if sh == 0 else pltpu.roll(win, sh, 0)
            acc = acc + src * cw_ref[i:i + 1, :]
        xc[pl.ds(r0, ch), :] = acc[CONV_PAD:CONV_PAD + ch]
        return 0
    lax.fori_loop(0, n, conv, 0)

    def gate(xb, w_ref, b_ref, d):
        z = jnp.concatenate([_dot(xb[:, j * LANES:(j + 1) * LANES], w_ref[d, j])
                             for j in range(width // LANES)], axis=1) + b_ref[d:d + 1, :]
        return 0.5 + 0.5 * jnp.tanh(0.5 * z)

    def gates(r0, d):
        x = xc[pl.ds(r0, ch), :]
        xb = x.astype(BF16)
        r = gate(xb, wa_ref, ba_ref, d)
        i = gate(xb, wi_ref, bi_ref, d)
        a = jnp.exp2((-LRU_C * LOG2_E * _softplus(-lam_ref[d:d + 1, :])) * r)
        y = 1.0 - a * a
        u = jnp.where(y > 0.0, y * lax.rsqrt(y), 0.0) * (i * x)
        return a, u

    def step(c, carry, final):
        h_f, h_b = carry
        rows = (pl.multiple_of(c * ch, ch), pl.multiple_of((n - 1 - c) * ch, ch))
        new = []
        for d, (r0, h) in enumerate(zip(rows, (h_f, h_b))):
            a, u = gates(r0, d)

            def store(g, v, r0=r0):
                at = pl.ds(r0 + g * SUBLANES, SUBLANES)
                acc_ref[at, :] = v + acc_ref[at, :] if final else v
            new.append(_scan_chunk(a, u, h, bool(d), store))
        if final:
            for r0 in rows:
                y = yr_ref[0, pl.ds(r0, ch), :].astype(F32)
                o_ref[0, pl.ds(r0, ch), :] = (acc_ref[pl.ds(r0, ch), :]
                                              * _gelu_tanh(y)).astype(o_ref.dtype)
        return tuple(new)
    zero = jnp.zeros((1, width), F32)
    carry = lax.fori_loop(0, n // 2, functools.partial(step, final=False), (zero, zero))
    lax.fori_loop(n // 2, n, functools.partial(step, final=True), carry)


def _lru(xr, yr, conv_w, conv_b, wa, wi, ba, bi, lam):
    b, s, c = xr.shape
    per = 2 if (c // LANES) % 2 == 0 else 1
    width = per * LANES
    assert (s // LRU_CHUNK) % 2 == 0
    seq = lambda: pl.BlockSpec((1, s, width), lambda i, j: (i, 0, j))
    chan = lambda rows: pl.BlockSpec((rows, width), lambda i, j: (0, j))
    wspec = lambda: pl.BlockSpec((2, per, LANES, LANES), lambda i, j: (0, j, 0, 0))
    return pl.pallas_call(
        _lru_kernel,
        out_shape=jax.ShapeDtypeStruct((b, s, c), BF16),
        grid=(b, c // width),
        in_specs=[seq(), seq(), chan(conv_w.shape[0]), chan(1), wspec(), wspec(),
                  chan(2), chan(2), chan(2)],
        out_specs=seq(),
        scratch_shapes=[pltpu.VMEM((s + 2 * CONV_PAD, width), BF16),
                        pltpu.VMEM((s, width), F32),
                        pltpu.VMEM((s, width), F32)],
        compiler_params=_params(("arbitrary", "arbitrary")),
        name="lru",
    )(xr, yr, conv_w, conv_b.reshape(1, c), wa, wi, ba, bi, lam)


def _gla_kernel(q_ref, k_ref, v_ref, r_ref, lr_ref, w2_ref, b2_ref, gn_ref, o_ref, acc):
    s = q_ref.shape[1]
    dk = q_ref.shape[2]
    dv = v_ref.shape[2]
    ck = GLA_CHUNK
    n = s // ck
    half = n // 2
    scale = dk ** -0.5
    row = lax.broadcasted_iota(I32, (ck, ck), 0)
    col = lax.broadcasted_iota(I32, (ck, ck), 1)
    tri_f = (row >= col).astype(BF16)
    tri_b = (row <= col).astype(BF16)
    mask_f = row >= col
    mask_b = col > row
    lane2 = lax.broadcasted_iota(I32, (2 * ck, 2 * ck), 1)
    zeros_v = jnp.zeros((ck, dv), BF16)

    def finish(r0, o):
        y = _rms(o) * gn_ref[...]
        rr = r_ref[0, pl.ds(r0, ck), :].astype(F32)
        o_ref[0, pl.ds(r0, ck), :] = (y * _silu(rr)).astype(o_ref.dtype)

    unroll = max(u for u in (1, 2, 4, 8) if half % u == 0)

    def group(it, carry, final):
        states = list(carry)
        jobs = []
        for uu in range(unroll):
            i = it * unroll + uu
            jobs += [(i, 0), (n - 1 - i, 1)]
        rows = [pl.multiple_of(c * ck, ck) for c, _ in jobs]
        pre = [_dot(lr_ref[0, pl.ds(r0, ck), :], w2_ref[d]) + b2_ref[d:d + 1, :]
               for (_, d), r0 in zip(jobs, rows)]
        logb = []
        for (_, d), x in zip(jobs, pre):
            g_hi, g_lo = _split_bf16(-_softplus(-x) * (1.0 / GLA_TAU))
            tri = tri_b if d else tri_f
            logb.append(_dot(tri, g_hi) + _dot(tri, g_lo))
        q_dec, att, k_t, decay = [], [], [], []
        for (_, d), r0, b in zip(jobs, rows, logb):
            q = q_ref[0, pl.ds(r0, ck), :].astype(F32) * scale
            k = k_ref[0, pl.ds(r0, ck), :].astype(F32)
            qd = (q * jnp.exp(b)).astype(BF16)
            kd = (k * jnp.exp(-b)).astype(BF16)
            a = lax.dot_general(qd, kd, (((1,), (1,)), ((), ())), preferred_element_type=F32)
            b_edge = b[0:1] if d else b[ck - 1:ck]
            stacked_t = jnp.concatenate([k * jnp.exp(b_edge - b), b], axis=0).T
            edge_lane = ck if d else 2 * ck - 1
            q_dec.append(qd)
            att.append(jnp.where(mask_b if d else mask_f, a, 0.0).astype(BF16))
            decay.append(jnp.exp(stacked_t[:, edge_lane:edge_lane + 1]))
            k_t.append(jnp.where(lane2 < ck, stacked_t, 0.0).astype(BF16))
        intra, inject = [], []
        for r0, a, kt in zip(rows, att, k_t):
            v = v_ref[0, pl.ds(r0, ck), :]
            intra.append(_dot(a, v))
            inject.append(_dot(kt, jnp.concatenate([v, zeros_v], axis=0)))
        seen = []
        for (_, d), dc, inj in zip(jobs, decay, inject):
            seen.append(states[d].astype(BF16))
            states[d] = states[d] * dc + inj
        for r0, qd, st, o in zip(rows, q_dec, seen, intra):
            o = o + _dot(qd, st)
            if final:
                finish(r0, o + acc[pl.ds(r0, ck), :])
            else:
                acc[pl.ds(r0, ck), :] = o
        return tuple(states)

    zero = jnp.zeros((dk, dv), F32)
    carry = lax.fori_loop(0, half // unroll, functools.partial(group, final=False), (zero, zero))
    lax.fori_loop(half // unroll, n // unroll, functools.partial(group, final=True), carry)


def _gla(q, k, v, r, lr, w2p, b2, gnorm):
    b, s, dkt = q.shape
    dvt = v.shape[2]
    h = GLA_HEADS
    dk, dv = dkt // h, dvt // h
    nlr = lr.shape[2]
    kspec = lambda: pl.BlockSpec((1, s, dk), lambda i, j: (i, 0, j))
    vspec = lambda: pl.BlockSpec((1, s, dv), lambda i, j: (i, 0, j))
    return pl.pallas_call(
        _gla_kernel,
        out_shape=jax.ShapeDtypeStruct((b, s, dvt), BF16),
        grid=(b, h),
        in_specs=[kspec(), kspec(), vspec(), vspec(),
                  pl.BlockSpec((1, s, nlr), lambda i, j: (i, 0, 0)),
                  pl.BlockSpec((2, nlr, dk), lambda i, j: (0, 0, j)),
                  pl.BlockSpec((2, dk), lambda i, j: (0, j)),
                  pl.BlockSpec((1, dv), lambda i, j: (0, j))],
        out_specs=vspec(),
        scratch_shapes=[pltpu.VMEM((s, dv), F32)],
        compiler_params=_params(("arbitrary", "arbitrary")),
        name="gla",
    )(q, k, v, r, lr, w2p, b2, gnorm.reshape(1, dvt))


def _group_reduce(x, lane, op):
    for sft in (1, 2, 4):
        up = pltpu.roll(x, LANES - sft, 1)
        dn = pltpu.roll(x, sft, 1)
        x = op(x, jnp.where((lane & sft) == 0, up, dn))
    return x


def _route(scores, sel, n_exp):
    neg = -jnp.inf
    lane = lax.broadcasted_iota(I32, scores.shape, 1)
    lane_f = lane.astype(F32)
    per_group = n_exp // N_GROUPS
    assert per_group == SUBLANES and n_exp <= LANES
    valid = lane < n_exp
    v = jnp.where(valid, sel, neg)
    m1 = _group_reduce(v, lane, jnp.maximum)
    first = _group_reduce(jnp.where(v == m1, lane_f, float(LANES)), lane, jnp.minimum)
    m2 = _group_reduce(jnp.where(lane_f == first, neg, v), lane, jnp.maximum)
    gs = m1 + m2
    gidx = lane >> 3
    n_slots = LANES // per_group
    rank = jnp.zeros(scores.shape, F32)
    for kk in range(1, n_slots):
        other = pltpu.roll(gs, per_group * kk, 1)
        og = (gidx - kk) & (n_slots - 1)
        better = jnp.where(other > gs, 1.0, jnp.where(other == gs, jnp.where(og < gidx, 1.0, 0.0), 0.0))
        rank = rank + better
    w = jnp.where(rank < float(TOPK_GROUPS), v, neg)
    cols = []
    chosen = jnp.zeros(scores.shape, F32)
    for _ in range(TOP_K):
        m = jnp.max(w, axis=1, keepdims=True)
        j = jnp.min(jnp.where(w == m, lane_f, float(LANES)), axis=1, keepdims=True)
        pick = lane_f == j
        w = jnp.where(pick, neg, w)
        chosen = jnp.where(pick, 1.0, chosen)
        cols.append(j)
    return cols, chosen, lane_f


def _merge_kernel(n_exp, x_ref, yl_ref, yg_ref, gl_ref, gg_ref, mod_ref, n2_ref, mb_ref,
                  pl_ref, pg_ref, wo_ref, rw_ref, rb_ref, sg_ref, su_ref, sd_ref,
                  base_ref, h_ref, p_ref, w_ref, cnt_ref, ls_ref, rb4_ref, tot_ref, run, lg):
    tm = x_ref.shape[0]
    i = pl.program_id(0)

    @pl.when(i == 0)
    def _():
        run[...] = jnp.zeros_like(run)
        lg[...] = jnp.zeros_like(lg)

    prev_logits = lg[...]
    a = _dot(yl_ref[...], pl_ref[...])
    b = _dot(yg_ref[...], pg_ref[...])
    merged = (_sigmoid(gl_ref[...].astype(F32) + mb_ref[0:1, :]) * a
              + _sigmoid(gg_ref[...].astype(F32) + mb_ref[1:2, :]) * b)
    mix = _dot(merged.astype(BF16), wo_ref[...])
    x1 = x_ref[...] + mod_ref[0, 2:3, :] * mix
    h = _rms(x1) * n2_ref[...]
    h = h * (1.0 + mod_ref[0, 4:5, :]) + mod_ref[0, 3:4, :]
    hb = h.astype(BF16)
    h_ref[...] = hb
    shared = _dot(( _silu(_dot(hb, sg_ref[...])) * _dot(hb, su_ref[...]) ).astype(BF16), sd_ref[...])
    base_ref[...] = x1 + mod_ref[0, 5:6, :] * shared

    lg[...] = _dot3(h, rw_ref[...])

    scores = _sigmoid(prev_logits)
    cols, chosen, lane_f = _route(scores, scores + rb_ref[...], n_exp)
    picked = jnp.where(chosen > 0.0, scores, 0.0)
    cw = picked * (ROUTED_SCALE / jnp.sum(picked, axis=1, keepdims=True))
    chosen = chosen * jnp.where(i > 0, 1.0, 0.0)

    r = lax.broadcasted_iota(I32, (tm, tm), 0)
    c = lax.broadcasted_iota(I32, (tm, tm), 1)
    before = (c < r).astype(BF16)
    rank = _dot(before, chosen.astype(BF16))
    cnt = rank[tm - 1:tm, :] + chosen[tm - 1:tm, :]
    cnt8 = jnp.floor((cnt + (SUBLANES - 1)) * (1.0 / SUBLANES)) * SUBLANES
    lr_ = lax.broadcasted_iota(I32, (LANES, LANES), 0)
    lc_ = lax.broadcasted_iota(I32, (LANES, LANES), 1)
    lstart = _dot(jnp.broadcast_to(cnt8, (SUBLANES, LANES)).astype(BF16),
                  (lr_ < lc_).astype(BF16))[0:1]
    pos = rank + lstart
    cnt_ref[0] = cnt8.astype(I32)
    ls_ref[0] = lstart.astype(I32)
    rb4_ref[0] = run[...].astype(I32)
    run[...] = run[...] + cnt8
    tot_ref[...] = run[...].astype(I32)

    p_out = jnp.zeros(scores.shape, F32)
    w_out = jnp.zeros(scores.shape, F32)
    for kk, j in enumerate(cols):
        hit = lane_f == j
        pk = jnp.sum(jnp.where(hit, pos, 0.0), axis=1, keepdims=True)
        wk = jnp.sum(jnp.where(hit, cw, 0.0), axis=1, keepdims=True)
        slot = lane_f == float(kk)
        p_out = jnp.where(slot, pk, p_out)
        w_out = jnp.where(slot, wk, w_out)
    p_ref[...] = p_out.astype(I32)
    w_ref[...] = w_out


def _merge(x2, y_lru, y_gla, gl, gg, mod, norm2_g, merge_b, proj_lru, proj_gla, w_out,
           router_w, router_bias, sh_gate, sh_up, sh_down, seq):
    t, d = x2.shape
    tm = TM_MERGE
    per_seq = seq // tm
    n6 = mod.shape[1]
    n_exp = router_w.shape[1]
    rw = jnp.pad(router_w, ((0, 0), (0, LANES - n_exp)))
    rb = jnp.pad(router_bias.reshape(1, n_exp), ((0, 0), (0, LANES - n_exp)))
    n_tiles = t // tm
    proj = lambda i: jnp.minimum(i, n_tiles - 1)
    routed = lambda i: jnp.maximum(i - 1, 0)
    tok = lambda width: pl.BlockSpec((tm, width), lambda i: (proj(i), 0))
    in_specs = [tok(d), tok(y_lru.shape[1]), tok(y_gla.shape[1]), tok(d), tok(d),
                pl.BlockSpec((1, n6, d), lambda i: (proj(i) // per_seq, 0, 0)),
                _const_spec((1, d)), _const_spec((2, d)),
                _const_spec(proj_lru.shape), _const_spec(proj_gla.shape), _const_spec(w_out.shape),
                _const_spec(rw.shape), _const_spec(rb.shape),
                _const_spec(sh_gate.shape), _const_spec(sh_up.shape), _const_spec(sh_down.shape)]
    meta = jax.ShapeDtypeStruct((n_tiles, 1, LANES), I32)
    meta_spec = lambda: pl.BlockSpec((1, 1, LANES), lambda i: (routed(i), 0, 0))
    plan = lambda: pl.BlockSpec((tm, LANES), lambda i: (routed(i), 0))
    out_shape = [jax.ShapeDtypeStruct((t, d), F32), jax.ShapeDtypeStruct((t, d), BF16),
                 jax.ShapeDtypeStruct((t, LANES), I32), jax.ShapeDtypeStruct((t, LANES), F32),
                 meta, meta, meta, jax.ShapeDtypeStruct((1, LANES), I32)]
    out_specs = [tok(d), tok(d), plan(), plan(), meta_spec(), meta_spec(), meta_spec(),
                 pl.BlockSpec((1, LANES), lambda i: (0, 0))]
    return pl.pallas_call(
        functools.partial(_merge_kernel, n_exp),
        out_shape=out_shape,
        grid=(n_tiles + 1,),
        in_specs=in_specs,
        out_specs=out_specs,
        scratch_shapes=[pltpu.VMEM((1, LANES), F32), pltpu.VMEM((tm, LANES), F32)],
        compiler_params=_params(("arbitrary",)),
        name="merge",
    )(x2, y_lru, y_gla, gl, gg, mod, norm2_g.reshape(1, d), merge_b, proj_lru, proj_gla, w_out,
      rw, rb, sh_gate, sh_up, sh_down)


def _run_copies(n, src_ref, src0, dst_ref, dst0, sem, max_rows, wait=False):
    def piece(b):
        size = 1 << b

        @pl.when((n & size) != 0)
        def _():
            off = (n >> (b + 1)) << (b + 1)
            cp = pltpu.make_async_copy(
                src_ref.at[pl.ds(pl.multiple_of(src0 + off, SUBLANES), size)],
                dst_ref.at[pl.ds(pl.multiple_of(dst0 + off, SUBLANES), size)], sem)
            if wait:
                cp.wait()
            else:
                cp.start()

    bits = range(max_rows.bit_length() - 1, 2, -1)
    long_bits = [b for b in bits if (1 << b) >= LONG_RUN]

    @pl.when(n >= LONG_RUN)
    def _():
        for b in long_bits:
            piece(b)
    for b in bits:
        if b not in long_bits:
            piece(b)


def _row_index_bf16(tm):
    assert tm <= 256
    return lax.broadcasted_iota(I32, (tm, tm), 0).astype(F32).astype(BF16)


def _pack(x):
    half = x.shape[1] // 2
    bits = lax.bitcast_convert_type(x.astype(BF16).astype(F32), jnp.uint32)
    return (bits[:, :half] >> 16) | (bits[:, half:] & jnp.uint32(0xFFFF0000))


def _pack_bf16_valued(x):
    half = x.shape[1] // 2
    bits = lax.bitcast_convert_type(x, jnp.uint32)
    return (bits[:, :half] >> 16) | bits[:, half:]


def _unpack(w):
    lo = lax.bitcast_convert_type(w << 16, F32)
    hi = lax.bitcast_convert_type(w & jnp.uint32(0xFFFF0000), F32)
    return lo.astype(BF16), hi.astype(BF16)


def _stage_rows(tm, n_exp):
    return -(-(TOP_K * tm + SUBLANES * n_exp) // tm) * tm


def _dispatch_kernel(n_exp, pad_ref, cnt_ref, ls_ref, dst_ref, h_ref, p_ref, xs_ref,
                     stage, zbuf, sent, sems):
    i = pl.program_id(0)
    n = pl.num_programs(0)
    tm = h_ref.shape[0]
    rows = stage.shape[1]
    top = 1 << (rows.bit_length() - 1)
    slot = i % 2
    total = ls_ref[0, 0, n_exp]

    def drain(sl):
        _run_copies(sent[sl], stage.at[sl], 0, xs_ref, 0, sems.at[sl], top, wait=True)

    @pl.when(i >= 2)
    def _():
        drain(slot)

    pt = p_ref[...].astype(F32).T
    r = _row_index_bf16(tm)

    def sort_rows(c):
        sel = jnp.zeros((tm, tm), BF16)
        for kk in range(TOP_K):
            sel = jnp.where(r == (pt[kk:kk + 1, :] - float(c * tm)).astype(BF16), 1.0, sel)
        stage[slot, c * tm:(c + 1) * tm, :] = _pack_bf16_valued(_dot(sel, h_ref[...]))

    for c in range(rows // tm):
        if c < TOP_K:
            sort_rows(c)
        else:
            pl.when(c * tm < total)(functools.partial(sort_rows, c))

    def per_expert(e, _):
        _run_copies(cnt_ref[0, 0, e], stage.at[slot], ls_ref[0, 0, e], xs_ref, dst_ref[0, 0, e],
                    sems.at[slot], tm)
        return 0
    lax.fori_loop(0, n_exp, per_expert, 0)
    sent[slot] = total

    @pl.when(i == n - 1)
    def _():
        drain(slot)

        @pl.when(n >= 2)
        def _():
            drain(1 - slot)
        blk = zbuf.shape[0]
        zbuf[...] = jnp.zeros_like(zbuf)
        n_blocks = xs_ref.shape[0] // blk

        def zero_fill(wait):
            def per_e(e, _):
                _run_copies(pad_ref[n_exp + e], zbuf, 0, xs_ref, pad_ref[e], sems.at[2], blk // 2, wait)
                return 0
            lax.fori_loop(0, n_exp, per_e, 0)

            def per_blk(bi, _):
                cp = pltpu.make_async_copy(zbuf, xs_ref.at[pl.ds(pl.multiple_of(bi * blk, blk), blk)],
                                           sems.at[2])
                if wait:
                    cp.wait()
                else:
                    cp.start()
                return 0
            lax.fori_loop(pad_ref[2 * n_exp], n_blocks, per_blk, 0)
        zero_fill(False)
        zero_fill(True)


def _expert_block(t, n_exp):
    mean_rows = t * TOP_K // n_exp
    blk = EXPERT_BLOCK_MIN
    while blk < EXPERT_BLOCK_MAX and blk * 8 <= mean_rows:
        blk *= 2
    return blk


def _dispatch(pad, cnt_t, ls_t, dst_t, h, p_out, n_rows, n_exp, blk):
    t, d = h.shape
    tm = TM_MERGE
    meta = lambda: pl.BlockSpec((1, 1, LANES), lambda i, pd: (i, 0, 0), memory_space=pltpu.SMEM)
    grid_spec = pltpu.PrefetchScalarGridSpec(
        num_scalar_prefetch=1,
        grid=(t // tm,),
        in_specs=[meta(), meta(), meta(),
                  pl.BlockSpec((tm, d), lambda i, pd: (i, 0)),
                  pl.BlockSpec((tm, LANES), lambda i, pd: (i, 0))],
        out_specs=pl.BlockSpec(memory_space=pl.ANY),
        scratch_shapes=[pltpu.VMEM((2, _stage_rows(tm, n_exp), d // 2), jnp.uint32),
                        pltpu.VMEM((blk, d // 2), jnp.uint32),
                        pltpu.SMEM((2,), I32),
                        pltpu.SemaphoreType.DMA((3,))],
    )
    return pl.pallas_call(
        functools.partial(_dispatch_kernel, n_exp),
        out_shape=jax.ShapeDtypeStruct((n_rows, d // 2), jnp.uint32),
        grid_spec=grid_spec,
        compiler_params=_params(("arbitrary",)),
        name="dispatch",
    )(pad, cnt_t, ls_t, dst_t, h, p_out)


def _experts_kernel(be_ref, used_ref, xs_ref, wg_ref, wu_ref, wd_ref, ys_ref):
    del be_ref

    @pl.when(pl.program_id(0) >= used_ref[0])
    def _():
        ys_ref[...] = jnp.zeros_like(ys_ref)

    @pl.when(pl.program_id(0) < used_ref[0])
    def _():
        half = xs_ref.shape[1]
        lo, hi = _unpack(xs_ref[...])
        gate = _dot(lo, wg_ref[0, :half, :]) + _dot(hi, wg_ref[0, half:, :])
        up = _dot(lo, wu_ref[0, :half, :]) + _dot(hi, wu_ref[0, half:, :])
        ys_ref[...] = _pack(_dot((_silu(gate) * up).astype(BF16), wd_ref[0]))


def _experts(blk_expert, n_used, xs, wg, wu, wd, blk):
    n_rows, half = xs.shape
    d, de = wg.shape[1], wg.shape[2]
    row = lambda i, be, used: (jnp.minimum(i, used[0] - 1), 0)
    wsel = lambda i, be, used: (be[jnp.minimum(i, used[0] - 1)], 0, 0)
    grid_spec = pltpu.PrefetchScalarGridSpec(
        num_scalar_prefetch=2,
        grid=(n_rows // blk,),
        in_specs=[pl.BlockSpec((blk, half), row),
                  pl.BlockSpec((1, d, de), wsel),
                  pl.BlockSpec((1, d, de), wsel),
                  pl.BlockSpec((1, de, d), wsel)],
        out_specs=pl.BlockSpec((blk, half), lambda i, be, used: (i, 0)),
    )
    return pl.pallas_call(
        _experts_kernel,
        out_shape=jax.ShapeDtypeStruct((n_rows, half), jnp.uint32),
        grid_spec=grid_spec,
        compiler_params=_params(("arbitrary",)),
        name="experts",
    )(blk_expert, n_used, xs, wg, wu, wd)


def _combine_kernel(n_exp, cnt_ref, ls_ref, src_ref, cnt1_ref, ls1_ref, src1_ref, base_ref, p_ref,
                    w_ref, mod_ref, fg_ref, ys_ref, o_ref, stage, acc, sems):
    i = pl.program_id(0)
    n = pl.num_programs(0)
    tm = base_ref.shape[0]
    rows = stage.shape[1]
    half = stage.shape[2]
    top = 1 << (rows.bit_length() - 1)
    slot = i % 2
    total = ls_ref[0, 0, n_exp]

    def gather(c_ref, l_ref, s_ref, sl):
        def per_expert(e, _):
            _run_copies(c_ref[0, 0, e], ys_ref, s_ref[0, 0, e], stage.at[sl], l_ref[0, 0, e],
                        sems.at[sl], tm)
            return 0
        lax.fori_loop(0, n_exp, per_expert, 0)

    @pl.when(i == 0)
    def _():
        stage[...] = jnp.zeros_like(stage)
        gather(cnt_ref, ls_ref, src_ref, 0)

    @pl.when(i + 1 < n)
    def _():
        gather(cnt1_ref, ls1_ref, src1_ref, 1 - slot)

    _run_copies(total, ys_ref, 0, stage.at[slot], 0, sems.at[slot], top, wait=True)

    pt = p_ref[...].astype(F32).T
    wt = w_ref[...].T

    r = _row_index_bf16(tm)

    def add_rows(c):
        wb = jnp.zeros((tm, tm), BF16)
        for kk in range(TOP_K):
            wb = jnp.where(r == (pt[kk:kk + 1, :] - float(c * tm)).astype(BF16),
                           wt[kk:kk + 1, :].astype(BF16), wb)
        lo, hi = _unpack(stage[slot, c * tm:(c + 1) * tm, :])
        tdot = lambda x: lax.dot_general(wb, x, (((0,), (0,)), ((), ())), preferred_element_type=F32)
        if c == 0:
            acc[:, :half] = tdot(lo)
            acc[:, half:] = tdot(hi)
        else:
            acc[:, :half] += tdot(lo)
            acc[:, half:] += tdot(hi)

    for c in range(rows // tm):
        if c < TOP_K:
            add_rows(c)
        else:
            pl.when(c * tm < total)(functools.partial(add_rows, c))
    y = base_ref[...] + mod_ref[0, 5:6, :] * acc[...]
    o_ref[...] = _rms(y) * fg_ref[...]


def _combine(cnt_t, ls_t, src_t, base, p_out, w8, mod, final_g, ys, seq, n_exp):
    t, d = base.shape
    tm = TM_MERGE
    steps = t // tm
    per_seq = seq // tm
    n6 = mod.shape[1]
    cur = lambda: pl.BlockSpec((1, 1, LANES), lambda i: (i, 0, 0), memory_space=pltpu.SMEM)
    nxt = lambda: pl.BlockSpec((1, 1, LANES), lambda i: (jnp.minimum(i + 1, steps - 1), 0, 0),
                               memory_space=pltpu.SMEM)
    return pl.pallas_call(
        functools.partial(_combine_kernel, n_exp),
        out_shape=jax.ShapeDtypeStruct((t, d), F32),
        grid=(steps,),
        in_specs=[cur(), cur(), cur(), nxt(), nxt(), nxt(),
                  pl.BlockSpec((tm, d), lambda i: (i, 0)),
                  pl.BlockSpec((tm, LANES), lambda i: (i, 0)),
                  pl.BlockSpec((tm, LANES), lambda i: (i, 0)),
                  pl.BlockSpec((1, n6, d), lambda i: (i // per_seq, 0, 0)),
                  pl.BlockSpec((1, d), lambda i: (0, 0)),
                  pl.BlockSpec(memory_space=pl.ANY)],
        out_specs=pl.BlockSpec((tm, d), lambda i: (i, 0)),
        scratch_shapes=[pltpu.VMEM((2, _stage_rows(tm, n_exp), d // 2), jnp.uint32),
                        pltpu.VMEM((tm, d), F32),
                        pltpu.SemaphoreType.DMA((2,))],
        compiler_params=_params(("arbitrary",)),
        name="combine",
    )(cnt_t, ls_t, src_t, cnt_t, ls_t, src_t, base, p_out, w8, mod, final_g.reshape(1, d), ys)


def _prepare(l, ada_w, ada_b, norm1_g, w_in, conv_w, conv_b, lru_wa, lru_ba, lru_wi, lru_bi,
             lru_lambda, gla_w2, gla_b2, gla_norm_g, proj_lru, proj_gla, merge_b, w_out, norm2_g,
             router_w, router_bias, exp_w_gate, exp_w_up, exp_w_down, sh_w_gate, sh_w_up,
             sh_w_down):
    d = w_in.shape[1]
    d_rnn = conv_w.shape[2]
    dkt = gla_w2.shape[3]
    dvt = gla_norm_g.shape[1]
    rank = gla_w2.shape[2]
    sizes = (d_rnn, d_rnn, dkt, dkt, dvt, dvt, 2 * rank, d, d)
    wb = w_in[l].astype(BF16)
    parts, off = [], 0
    for sz in sizes:
        parts.append(wb[:, off:off + sz])
        off += sz
    w2 = gla_w2[l].astype(BF16)
    zeros = jnp.zeros_like(w2[0])
    w2p = jnp.stack([jnp.concatenate([w2[0], zeros], axis=0),
                     jnp.concatenate([zeros, w2[1]], axis=0)])
    return dict(
        ada_w=ada_w[l], ada_b=ada_b[l], norm1_g=norm1_g[l], w_parts=parts,
        conv_w=conv_w[l], conv_b=conv_b[l], wa=lru_wa[l].astype(BF16), wi=lru_wi[l].astype(BF16),
        ba=lru_ba[l], bi=lru_bi[l], lam=lru_lambda[l], w2p=w2p, b2=gla_b2[l],
        gnorm=gla_norm_g[l], proj_lru=proj_lru[l].astype(BF16), proj_gla=proj_gla[l].astype(BF16),
        merge_b=merge_b[l], w_out=w_out[l].astype(BF16), norm2_g=norm2_g[l],
        router_w=router_w[l], router_bias=router_bias[l],
        wg=exp_w_gate[l].astype(BF16), wu=exp_w_up[l].astype(BF16), wd=exp_w_down[l].astype(BF16),
        sg=sh_w_gate[l].astype(BF16), su=sh_w_up[l].astype(BF16), sd=sh_w_down[l].astype(BF16))


def _layer(x, c, p, final_g):
    bsz, s, d = x.shape
    t = bsz * s
    x2 = x.reshape(t, d)
    mod = _ada(c, p["ada_w"], p["ada_b"])
    xr, yr, q, k, v, r, lr, gl, gg = _inproj(x2, mod, p["norm1_g"], p["w_parts"], s)
    seqv = lambda a: a.reshape(bsz, s, a.shape[1])
    y_lru = _lru(seqv(xr), seqv(yr), p["conv_w"], p["conv_b"], p["wa"], p["wi"], p["ba"], p["bi"],
                 p["lam"]).reshape(t, -1)
    y_gla = _gla(seqv(q), seqv(k), seqv(v), seqv(r), seqv(lr), p["w2p"], p["b2"],
                 p["gnorm"]).reshape(t, -1)
    base, h, p_out, w8, cnt_t, ls_t, rb_t, counts = _merge(
        x2, y_lru, y_gla, gl, gg, mod, p["norm2_g"], p["merge_b"], p["proj_lru"], p["proj_gla"],
        p["w_out"], p["router_w"], p["router_bias"], p["sg"], p["su"], p["sd"], s)
    n_exp = p["router_w"].shape[1]
    blk = _expert_block(t, n_exp)
    n_blocks = -(-(t // TM_MERGE) * _stage_rows(TM_MERGE, n_exp) // blk) + n_exp
    cnt = counts[0, :n_exp]
    padded = (cnt + blk - 1) // blk * blk
    pend = jnp.cumsum(padded)
    pstart = (pend - padded).astype(I32)
    blk_expert = jnp.minimum(
        jnp.sum((pend[None, :] <= (jnp.arange(n_blocks, dtype=I32) * blk)[:, None]).astype(I32), axis=1),
        n_exp - 1).astype(I32)
    row_t = rb_t + jnp.pad(pstart, (0, LANES - n_exp))[None, None, :]
    n_used = (pend[-1:] // blk).astype(I32)
    pad = jnp.concatenate([pstart + cnt, padded - cnt, n_used]).astype(I32)
    xs = _dispatch(pad, cnt_t, ls_t, row_t, h, p_out, n_blocks * blk, n_exp, blk)
    ys = _experts(blk_expert, n_used, xs, p["wg"], p["wu"], p["wd"], blk)
    return _combine(cnt_t, ls_t, row_t, base, p_out, w8, mod, final_g, ys, s,
                    n_exp).reshape(bsz, s, d)


def kernel(x_prompt, x_sample, c_prompt, c_sample, ada_w, ada_b, norm1_g, w_in, conv_w, conv_b,
           lru_wa, lru_ba, lru_wi, lru_bi, lru_lambda, gla_w2, gla_b2, gla_norm_g, proj_lru,
           proj_gla, merge_b, w_out, norm2_g, router_w, router_bias, exp_w_gate, exp_w_up,
           exp_w_down, sh_w_gate, sh_w_up, sh_w_down, final_g):
    depth = ada_w.shape[0]
    assert depth == 1, "the fused final RMSNorm assumes a single layer"
    p = _prepare(0, ada_w, ada_b, norm1_g, w_in, conv_w, conv_b, lru_wa, lru_ba, lru_wi, lru_bi,
                 lru_lambda, gla_w2, gla_b2, gla_norm_g, proj_lru, proj_gla, merge_b, w_out,
                 norm2_g, router_w, router_bias, exp_w_gate, exp_w_up, exp_w_down, sh_w_gate,
                 sh_w_up, sh_w_down)
    return (_layer(x_prompt, c_prompt, p, final_g), _layer(x_sample, c_sample, p, final_g))
```

```python
import functools

import jax
import jax.numpy as jnp
from jax import lax
from jax.experimental import pallas as pl
from jax.experimental.pallas import tpu as pltpu

F32 = jnp.float32
BF16 = jnp.bfloat16
I32 = jnp.int32

EPS = 1e-6
LOG2_E = 1.4426950408889634
LRU_C = 8.0
GLA_HEADS = 4
GLA_TAU = 16.0
GLA_CHUNK = 64
N_GROUPS = 8
TOPK_GROUPS = 4
TOP_K = 8
ROUTED_SCALE = 2.5

LANES = 128
SUBLANES = 8
VMEM_LIMIT = 56 * 1024 * 1024

TM_INPROJ = 512
TM_MERGE = 256
EXPERT_BLOCK_MIN = 256
EXPERT_BLOCK_MAX = 1024
LRU_CHUNK = 256
CONV_PAD = 16


def _dot(a, b):
    return jnp.dot(a, b, preferred_element_type=F32)


def _split_bf16(a):
    hi = a.astype(BF16)
    lo = (a - hi.astype(F32)).astype(BF16)
    return hi, lo


def _dot3(a, b):
    a_hi, a_lo = _split_bf16(a)
    b_hi, b_lo = _split_bf16(b)
    return _dot(a_hi, b_hi) + _dot(a_lo, b_hi) + _dot(a_hi, b_lo)


def _sigmoid(x):
    return 1.0 / (1.0 + jnp.exp(-x))


def _silu(x):
    return x * _sigmoid(x)


def _softplus(z):
    return jnp.maximum(z, 0.0) + jnp.log(1.0 + jnp.exp(-jnp.abs(z)))


def _gelu_tanh(x):
    return 0.5 * x * (1.0 + jnp.tanh(0.7978845608028654 * (x + 0.044715 * (x * x * x))))


def _rms(x):
    return x * lax.rsqrt(jnp.mean(x * x, axis=-1, keepdims=True) + EPS)


def _const_spec(shape):
    zeros = (0,) * len(shape)
    return pl.BlockSpec(shape, lambda *_: zeros, pipeline_mode=pl.Buffered(1))


def _params(sem, vmem=VMEM_LIMIT):
    return pltpu.CompilerParams(dimension_semantics=sem, vmem_limit_bytes=vmem)


def _ada_kernel(c_ref, w_ref, b_ref, o_ref):
    o_ref[...] = _dot3(_silu(c_ref[...]), w_ref[...]) + b_ref[...]


def _ada(c, ada_w, ada_b):
    nb, d = c.shape
    nbp = -(-nb // SUBLANES) * SUBLANES
    cp = jnp.pad(c, ((0, nbp - nb), (0, 0)))
    n6 = ada_w.shape[1] // d
    out = pl.pallas_call(
        _ada_kernel,
        out_shape=jax.ShapeDtypeStruct((nbp, n6 * d), F32),
        grid=(n6,),
        in_specs=[pl.BlockSpec((nbp, d), lambda j: (0, 0)),
                  pl.BlockSpec((d, d), lambda j: (0, j)),
                  pl.BlockSpec((1, d), lambda j: (0, j))],
        out_specs=pl.BlockSpec((nbp, d), lambda j: (0, j)),
        compiler_params=_params(("arbitrary",)),
        name="ada",
    )(cp, ada_w, ada_b.reshape(1, -1))
    return out.reshape(nbp, n6, d)


def _inproj_kernel(x_ref, mod_ref, g_ref, *refs):
    n = len(refs) // 2
    w_refs, o_refs = refs[:n], refs[n:]
    h = _rms(x_ref[...]) * g_ref[...]
    h = h * (1.0 + mod_ref[0, 1:2, :]) + mod_ref[0, 0:1, :]
    hb = h.astype(BF16)
    for w_ref, o_ref in zip(w_refs, o_refs):
        o_ref[...] = _dot(hb, w_ref[...]).astype(o_ref.dtype)


def _inproj(x2, mod, norm_g, w_parts, seq):
    t, d = x2.shape
    tm = TM_INPROJ
    per_seq = seq // tm
    n6 = mod.shape[1]
    in_specs = [pl.BlockSpec((tm, d), lambda i: (i, 0)),
                pl.BlockSpec((1, n6, d), lambda i: (i // per_seq, 0, 0)),
                _const_spec((1, d))]
    in_specs += [_const_spec(w.shape) for w in w_parts]
    out_shape = [jax.ShapeDtypeStruct((t, w.shape[1]), BF16) for w in w_parts]
    out_specs = [pl.BlockSpec((tm, w.shape[1]), lambda i: (i, 0)) for w in w_parts]
    return pl.pallas_call(
        _inproj_kernel,
        out_shape=out_shape,
        grid=(t // tm,),
        in_specs=in_specs,
        out_specs=out_specs,
        compiler_params=_params(("arbitrary",)),
        name="inproj",
    )(x2, mod, norm_g.reshape(1, d), *w_parts)


def _scan_chunk(a, u, h, reverse, store):
    groups = a.shape[0] // SUBLANES
    a = a.reshape(groups, SUBLANES, a.shape[1])
    u = u.reshape(groups, SUBLANES, u.shape[1])
    rm = lax.broadcasted_iota(I32, a.shape, 1)
    for d in (1, 2, 4):
        shift = SUBLANES - d if reverse else d
        keep = (rm < SUBLANES - d) if reverse else (rm >= d)
        u = a * jnp.where(keep, pltpu.roll(u, shift, 1), 0.0) + u
        a = a * jnp.where(keep, pltpu.roll(a, shift, 1), 1.0)
    order = range(groups - 1, -1, -1) if reverse else range(groups)
    for g in order:
        hg = u[g] + a[g] * h
        store(g, hg)
        h = hg[0:1] if reverse else hg[SUBLANES - 1:SUBLANES]
    return h


def _lru_kernel(xr_ref, yr_ref, cw_ref, cb_ref, wa_ref, wi_ref, ba_ref, bi_ref, lam_ref,
                o_ref, xpad, xc, acc_ref):
    s = xr_ref.shape[1]
    width = xr_ref.shape[2]
    ch = LRU_CHUNK
    n = s // ch
    taps = cw_ref.shape[0]
    left = taps // 2
    zpad = jnp.zeros((CONV_PAD, width), xpad.dtype)
    xpad[0:CONV_PAD, :] = zpad
    xpad[CONV_PAD + s:CONV_PAD + s + CONV_PAD, :] = zpad

    def fill(c, _):
        r0 = pl.multiple_of(c * ch, ch)
        xpad[pl.ds(CONV_PAD + r0, ch), :] = xr_ref[0, pl.ds(r0, ch), :]
        return 0
    lax.fori_loop(0, n, fill, 0)

    def conv(c, _):
        r0 = pl.multiple_of(c * ch, ch)
        win = xpad[pl.ds(r0, ch + 2 * CONV_PAD), :].astype(F32)
        acc = jnp.zeros_like(win) + cb_ref[...]
        for i in range(taps):
            sh = (left - i) % (ch + 2 * CONV_PAD)
            src = win if sh == 0 else pltpu.roll(win, sh, 0)
            acc = acc + src * cw_ref[i:i + 1, :]
        xc[pl.ds(r0, ch), :] = acc[CONV_PAD:CONV_PAD + ch]
        return 0
    lax.fori_loop(0, n, conv, 0)

    def gate(xb, w_ref, b_ref, d):
        z = jnp.concatenate([_dot(xb[:, j * LANES:(j + 1) * LANES], w_ref[d, j])
                             for j in range(width // LANES)], axis=1) + b_ref[d:d + 1, :]
        return 0.5 + 0.5 * jnp.tanh(0.5 * z)

    def gates(r0, d):
        x = xc[pl.ds(r0, ch), :]
        xb = x.astype(BF16)
        r = gate(xb, wa_ref, ba_ref, d)
        i = gate(xb, wi_ref, bi_ref, d)
        a = jnp.exp2((-LRU_C * LOG2_E * _softplus(-lam_ref[d:d + 1, :])) * r)
        y = 1.0 - a * a
        u = jnp.where(y > 0.0, y * lax.rsqrt(y), 0.0) * (i * x)
        return a, u

    def step(c, carry, final):
        h_f, h_b = carry
        rows = (pl.multiple_of(c * ch, ch), pl.multiple_of((n - 1 - c) * ch, ch))
        new = []
        for d, (r0, h) in enumerate(zip(rows, (h_f, h_b))):
            a, u = gates(r0, d)

            def store(g, v, r0=r0):
                at = pl.ds(r0 + g * SUBLANES, SUBLANES)
                acc_ref[at, :] = v + acc_ref[at, :] if final else v
            new.append(_scan_chunk(a, u, h, bool(d), store))
        if final:
            for r0 in rows:
                y = yr_ref[0, pl.ds(r0, ch), :].astype(F32)
                o_ref[0, pl.ds(r0, ch), :] = (acc_ref[pl.ds(r0, ch), :]
                                              * _gelu_tanh(y)).astype(o_ref.dtype)
        return tuple(new)
    zero = jnp.zeros((1, width), F32)
    carry = lax.fori_loop(0, n // 2, functools.partial(step, final=False), (zero, zero))
    lax.fori_loop(n // 2, n, functools.partial(step, final=True), carry)


def _lru(xr, yr, conv_w, conv_b, wa, wi, ba, bi, lam):
    b, s, c = xr.shape
    per = 2 if (c // LANES) % 2 == 0 else 1
    width = per * LANES
    assert (s // LRU_CHUNK) % 2 == 0
    seq = lambda: pl.BlockSpec((1, s, width), lambda i, j: (i, 0, j))
    chan = lambda rows: pl.BlockSpec((rows, width), lambda i, j: (0, j))
    wspec = lambda: pl.BlockSpec((2, per, LANES, LANES), lambda i, j: (0, j, 0, 0))
    return pl.pallas_call(
        _lru_kernel,
        out_shape=jax.ShapeDtypeStruct((b, s, c), BF16),
        grid=(b, c // width),
        in_specs=[seq(), seq(), chan(conv_w.shape[0]), chan(1), wspec(), wspec(),
                  chan(2), chan(2), chan(2)],
        out_specs=seq(),
        scratch_shapes=[pltpu.VMEM((s + 2 * CONV_PAD, width), BF16),
                        pltpu.VMEM((s, width), F32),
                        pltpu.VMEM((s, width), F32)],
        compiler_params=_params(("arbitrary", "arbitrary")),
        name="lru",
    )(xr, yr, conv_w, conv_b.reshape(1, c), wa, wi, ba, bi, lam)


def _gla_kernel(q_ref, k_ref, v_ref, r_ref, lr_ref, w2_ref, b2_ref, gn_ref, o_ref, acc):
    s = q_ref.shape[1]
    dk = q_ref.shape[2]
    dv = v_ref.shape[2]
    ck = GLA_CHUNK
    n = s // ck
    half = n // 2
    scale = dk ** -0.5
    row = lax.broadcasted_iota(I32, (ck, ck), 0)
    col = lax.broadcasted_iota(I32, (ck, ck), 1)
    tri_f = (row >= col).astype(BF16)
    tri_b = (row <= col).astype(BF16)
    mask_f = row >= col
    mask_b = col > row
    lane2 = lax.broadcasted_iota(I32, (2 * ck, 2 * ck), 1)
    zeros_v = jnp.zeros((ck, dv), BF16)

    def finish(r0, o):
        y = _rms(o) * gn_ref[...]
        rr = r_ref[0, pl.ds(r0, ck), :].astype(F32)
        o_ref[0, pl.ds(r0, ck), :] = (y * _silu(rr)).astype(o_ref.dtype)

    unroll = max(u for u in (1, 2, 4, 8) if half % u == 0)

    def group(it, carry, final):
        states = list(carry)
        jobs = []
        for uu in range(unroll):
            i = it * unroll + uu
            jobs += [(i, 0), (n - 1 - i, 1)]
        rows = [pl.multiple_of(c * ck, ck) for c, _ in jobs]
        pre = [_dot(lr_ref[0, pl.ds(r0, ck), :], w2_ref[d]) + b2_ref[d:d + 1, :]
               for (_, d), r0 in zip(jobs, rows)]
        logb = []
        for (_, d), x in zip(jobs, pre):
            g_hi, g_lo = _split_bf16(-_softplus(-x) * (1.0 / GLA_TAU))
            tri = tri_b if d else tri_f
            logb.append(_dot(tri, g_hi) + _dot(tri, g_lo))
        q_dec, att, k_t, decay = [], [], [], []
        for (_, d), r0, b in zip(jobs, rows, logb):
            q = q_ref[0, pl.ds(r0, ck), :].astype(F32) * scale
            k = k_ref[0, pl.ds(r0, ck), :].astype(F32)
            qd = (q * jnp.exp(b)).astype(BF16)
            kd = (k * jnp.exp(-b)).astype(BF16)
            a = lax.dot_general(qd, kd, (((1,), (1,)), ((), ())), preferred_element_type=F32)
            b_edge = b[0:1] if d else b[ck - 1:ck]
            stacked_t = jnp.concatenate([k * jnp.exp(b_edge - b), b], axis=0).T
            edge_lane = ck if d else 2 * ck - 1
            q_dec.append(qd)
            att.append(jnp.where(mask_b if d else mask_f, a, 0.0).astype(BF16))
            decay.append(jnp.exp(stacked_t[:, edge_lane:edge_lane + 1]))
            k_t.append(jnp.where(lane2 < ck, stacked_t, 0.0).astype(BF16))
        intra, inject = [], []
        for r0, a, kt in zip(rows, att, k_t):
            v = v_ref[0, pl.ds(r0, ck), :]
            intra.append(_dot(a, v))
            inject.append(_dot(kt, jnp.concatenate([v, zeros_v], axis=0)))
        seen = []
        for (_, d), dc, inj in zip(jobs, decay, inject):
            seen.append(states[d].astype(BF16))
            states[d] = states[d] * dc + inj
        for r0, qd, st, o in zip(rows, q_dec, seen, intra):
            o = o + _dot(qd, st)
            if final:
                finish(r0, o + acc[pl.ds(r0, ck), :])
            else:
                acc[pl.ds(r0, ck), :] = o
        return tuple(states)

    zero = jnp.zeros((dk, dv), F32)
    carry = lax.fori_loop(0, half // unroll, functools.partial(group, final=False), (zero, zero))
    lax.fori_loop(half // unroll, n // unroll, functools.partial(group, final=True), carry)


def _gla(q, k, v, r, lr, w2p, b2, gnorm):
    b, s, dkt = q.shape
    dvt = v.shape[2]
    h = GLA_HEADS
    dk, dv = dkt // h, dvt // h
    nlr = lr.shape[2]
    kspec = lambda: pl.BlockSpec((1, s, dk), lambda i, j: (i, 0, j))
    vspec = lambda: pl.BlockSpec((1, s, dv), lambda i, j: (i, 0, j))
    return pl.pallas_call(
        _gla_kernel,
        out_shape=jax.ShapeDtypeStruct((b, s, dvt), BF16),
        grid=(b, h),
        in_specs=[kspec(), kspec(), vspec(), vspec(),
                  pl.BlockSpec((1, s, nlr), lambda i, j: (i, 0, 0)),
                  pl.BlockSpec((2, nlr, dk), lambda i, j: (0, 0, j)),
                  pl.BlockSpec((2, dk), lambda i, j: (0, j)),
                  pl.BlockSpec((1, dv), lambda i, j: (0, j))],
        out_specs=vspec(),
        scratch_shapes=[pltpu.VMEM((s, dv), F32)],
        compiler_params=_params(("arbitrary", "arbitrary")),
        name="gla",
    )(q, k, v, r, lr, w2p, b2, gnorm.reshape(1, dvt))


def _group_reduce(x, lane, op):
    for sft in (1, 2, 4):
        up = pltpu.roll(x, LANES - sft, 1)
        dn = pltpu.roll(x, sft, 1)
        x = op(x, jnp.where((lane & sft) == 0, up, dn))
    return x


def _route(scores, sel, n_exp):
    neg = -jnp.inf
    lane = lax.broadcasted_iota(I32, scores.shape, 1)
    lane_f = lane.astype(F32)
    per_group = n_exp // N_GROUPS
    assert per_group == SUBLANES and n_exp <= LANES
    valid = lane < n_exp
    v = jnp.where(valid, sel, neg)
    m1 = _group_reduce(v, lane, jnp.maximum)
    first = _group_reduce(jnp.where(v == m1, lane_f, float(LANES)), lane, jnp.minimum)
    m2 = _group_reduce(jnp.where(lane_f == first, neg, v), lane, jnp.maximum)
    gs = m1 + m2
    gidx = lane >> 3
    n_slots = LANES // per_group
    rank = jnp.zeros(scores.shape, F32)
    for kk in range(1, n_slots):
        other = pltpu.roll(gs, per_group * kk, 1)
        og = (gidx - kk) & (n_slots - 1)
        better = jnp.where(other > gs, 1.0, jnp.where(other == gs, jnp.where(og < gidx, 1.0, 0.0), 0.0))
        rank = rank + better
    w = jnp.where(rank < float(TOPK_GROUPS), v, neg)
    cols = []
    chosen = jnp.zeros(scores.shape, F32)
    for _ in range(TOP_K):
        m = jnp.max(w, axis=1, keepdims=True)
        j = jnp.min(jnp.where(w == m, lane_f, float(LANES)), axis=1, keepdims=True)
        pick = lane_f == j
        w = jnp.where(pick, neg, w)
        chosen = jnp.where(pick, 1.0, chosen)
        cols.append(j)
    return cols, chosen, lane_f


def _merge_kernel(n_exp, x_ref, yl_ref, yg_ref, gl_ref, gg_ref, mod_ref, n2_ref, mb_ref,
                  pl_ref, pg_ref, wo_ref, rw_ref, rb_ref, sg_ref, su_ref, sd_ref,
                  base_ref, h_ref, p_ref, w_ref, cnt_ref, ls_ref, rb4_ref, tot_ref, run, lg):
    tm = x_ref.shape[0]
    i = pl.program_id(0)

    @pl.when(i == 0)
    def _():
        run[...] = jnp.zeros_like(run)
        lg[...] = jnp.zeros_like(lg)

    prev_logits = lg[...]
    a = _dot(yl_ref[...], pl_ref[...])
    b = _dot(yg_ref[...], pg_ref[...])
    merged = (_sigmoid(gl_ref[...].astype(F32) + mb_ref[0:1, :]) * a
              + _sigmoid(gg_ref[...].astype(F32) + mb_ref[1:2, :]) * b)
    mix = _dot(merged.astype(BF16), wo_ref[...])
    x1 = x_ref[...] + mod_ref[0, 2:3, :] * mix
    h = _rms(x1) * n2_ref[...]
    h = h * (1.0 + mod_ref[0, 4:5, :]) + mod_ref[0, 3:4, :]
    hb = h.astype(BF16)
    h_ref[...] = hb
    shared = _dot(( _silu(_dot(hb, sg_ref[...])) * _dot(hb, su_ref[...]) ).astype(BF16), sd_ref[...])
    base_ref[...] = x1 + mod_ref[0, 5:6, :] * shared

    lg[...] = _dot3(h, rw_ref[...])

    scores = _sigmoid(prev_logits)
    cols, chosen, lane_f = _route(scores, scores + rb_ref[...], n_exp)
    picked = jnp.where(chosen > 0.0, scores, 0.0)
    cw = picked * (ROUTED_SCALE / jnp.sum(picked, axis=1, keepdims=True))
    chosen = chosen * jnp.where(i > 0, 1.0, 0.0)

    r = lax.broadcasted_iota(I32, (tm, tm), 0)
    c = lax.broadcasted_iota(I32, (tm, tm), 1)
    before = (c < r).astype(BF16)
    rank = _dot(before, chosen.astype(BF16))
    cnt = rank[tm - 1:tm, :] + chosen[tm - 1:tm, :]
    cnt8 = jnp.floor((cnt + (SUBLANES - 1)) * (1.0 / SUBLANES)) * SUBLANES
    lr_ = lax.broadcasted_iota(I32, (LANES, LANES), 0)
    lc_ = lax.broadcasted_iota(I32, (LANES, LANES), 1)
    lstart = _dot(jnp.broadcast_to(cnt8, (SUBLANES, LANES)).astype(BF16),
                  (lr_ < lc_).astype(BF16))[0:1]
    pos = rank + lstart
    per_tile = 1.0 / SUBLANES
    cnt_ref[0] = (cnt8 * per_tile).astype(I32)
    ls_ref[0] = (lstart * per_tile).astype(I32)
    rb4_ref[0] = (run[...] * per_tile).astype(I32)
    run[...] = run[...] + cnt8
    tot_ref[...] = run[...].astype(I32)

    p_out = jnp.zeros(scores.shape, F32)
    w_out = jnp.zeros(scores.shape, F32)
    for kk, j in enumerate(cols):
        hit = lane_f == j
        pk = jnp.sum(jnp.where(hit, pos, 0.0), axis=1, keepdims=True)
        wk = jnp.sum(jnp.where(hit, cw, 0.0), axis=1, keepdims=True)
        slot = lane_f == float(kk)
        p_out = jnp.where(slot, pk, p_out)
        w_out = jnp.where(slot, wk, w_out)
    p_ref[...] = p_out.astype(I32)
    w_ref[...] = w_out


def _merge(x2, y_lru, y_gla, gl, gg, mod, norm2_g, merge_b, proj_lru, proj_gla, w_out,
           router_w, router_bias, sh_gate, sh_up, sh_down, seq):
    t, d = x2.shape
    tm = TM_MERGE
    per_seq = seq // tm
    n6 = mod.shape[1]
    n_exp = router_w.shape[1]
    rw = jnp.pad(router_w, ((0, 0), (0, LANES - n_exp)))
    rb = jnp.pad(router_bias.reshape(1, n_exp), ((0, 0), (0, LANES - n_exp)))
    n_tiles = t // tm
    proj = lambda i: jnp.minimum(i, n_tiles - 1)
    routed = lambda i: jnp.maximum(i - 1, 0)
    tok = lambda width: pl.BlockSpec((tm, width), lambda i: (proj(i), 0))
    in_specs = [tok(d), tok(y_lru.shape[1]), tok(y_gla.shape[1]), tok(d), tok(d),
                pl.BlockSpec((1, n6, d), lambda i: (proj(i) // per_seq, 0, 0)),
                _const_spec((1, d)), _const_spec((2, d)),
                _const_spec(proj_lru.shape), _const_spec(proj_gla.shape), _const_spec(w_out.shape),
                _const_spec(rw.shape), _const_spec(rb.shape),
                _const_spec(sh_gate.shape), _const_spec(sh_up.shape), _const_spec(sh_down.shape)]
    meta = jax.ShapeDtypeStruct((n_tiles, 1, LANES), I32)
    meta_spec = lambda: pl.BlockSpec((1, 1, LANES), lambda i: (routed(i), 0, 0))
    plan = lambda: pl.BlockSpec((tm, LANES), lambda i: (routed(i), 0))
    out_shape = [jax.ShapeDtypeStruct((t, d), F32), jax.ShapeDtypeStruct((t, d), BF16),
                 jax.ShapeDtypeStruct((t, LANES), I32), jax.ShapeDtypeStruct((t, LANES), F32),
                 meta, meta, meta, jax.ShapeDtypeStruct((1, LANES), I32)]
    out_specs = [tok(d), tok(d), plan(), plan(), meta_spec(), meta_spec(), meta_spec(),
                 pl.BlockSpec((1, LANES), lambda i: (0, 0))]
    return pl.pallas_call(
        functools.partial(_merge_kernel, n_exp),
        out_shape=out_shape,
        grid=(n_tiles + 1,),
        in_specs=in_specs,
        out_specs=out_specs,
        scratch_shapes=[pltpu.VMEM((1, LANES), F32), pltpu.VMEM((tm, LANES), F32)],
        compiler_params=_params(("arbitrary",)),
        name="merge",
    )(x2, y_lru, y_gla, gl, gg, mod, norm2_g.reshape(1, d), merge_b, proj_lru, proj_gla, w_out,
      rw, rb, sh_gate, sh_up, sh_down)


def _run_copies(n, src_ref, src0, dst_ref, dst0, sem, max_tiles, wait=False):
    for b in range(max_tiles.bit_length() - 1, -1, -1):
        size = 1 << b

        @pl.when((n & size) != 0)
        def _(b=b, size=size):
            off = (n >> (b + 1)) << (b + 1)
            cp = pltpu.make_async_copy(src_ref.at[pl.ds(src0 + off, size)],
                                       dst_ref.at[pl.ds(dst0 + off, size)], sem)
            if wait:
                cp.wait()
            else:
                cp.start()


def _row_index_bf16(tm):
    assert tm <= 256
    return lax.broadcasted_iota(I32, (tm, tm), 0).astype(F32).astype(BF16)


def _pack(x):
    half = x.shape[1] // 2
    bits = lax.bitcast_convert_type(x.astype(BF16).astype(F32), jnp.uint32)
    return (bits[:, :half] >> 16) | (bits[:, half:] & jnp.uint32(0xFFFF0000))


def _pack_bf16_valued(x):
    half = x.shape[1] // 2
    bits = lax.bitcast_convert_type(x, jnp.uint32)
    return (bits[:, :half] >> 16) | bits[:, half:]


def _unpack(w):
    lo = lax.bitcast_convert_type(w << 16, F32)
    hi = lax.bitcast_convert_type(w & jnp.uint32(0xFFFF0000), F32)
    return lo.astype(BF16), hi.astype(BF16)


def _stage_rows(tm, n_exp):
    return -(-(TOP_K * tm + SUBLANES * n_exp) // tm) * tm


def _dispatch_kernel(n_exp, pad_ref, cnt_ref, ls_ref, dst_ref, h_ref, p_ref, xs_ref,
                     stage, zbuf, sent, sems):
    i = pl.program_id(0)
    n = pl.num_programs(0)
    tm = h_ref.shape[0]
    tpc = tm // SUBLANES
    pieces = stage.shape[1] // tpc
    top = 1 << (stage.shape[1].bit_length() - 1)
    slot = i % 2
    total = ls_ref[0, 0, n_exp]

    def drain(sl):
        _run_copies(sent[sl], stage.at[sl], 0, xs_ref, 0, sems.at[sl], top, wait=True)

    @pl.when(i >= 2)
    def _():
        drain(slot)

    pt = p_ref[...].astype(F32).T
    r = _row_index_bf16(tm)

    def sort_rows(c):
        sel = jnp.zeros((tm, tm), BF16)
        for kk in range(TOP_K):
            sel = jnp.where(r == (pt[kk:kk + 1, :] - float(c * tm)).astype(BF16), 1.0, sel)
        packed = _pack_bf16_valued(_dot(sel, h_ref[...]))
        stage[slot, c * tpc:(c + 1) * tpc] = packed.reshape(tpc, SUBLANES, packed.shape[1])

    for c in range(pieces):
        if c < TOP_K:
            sort_rows(c)
        else:
            pl.when(c * tpc < total)(functools.partial(sort_rows, c))

    def per_expert(e, _):
        _run_copies(cnt_ref[0, 0, e], stage.at[slot], ls_ref[0, 0, e], xs_ref, dst_ref[0, 0, e],
                    sems.at[slot], tpc)
        return 0
    lax.fori_loop(0, n_exp, per_expert, 0)
    sent[slot] = total

    @pl.when(i == n - 1)
    def _():
        drain(slot)

        @pl.when(n >= 2)
        def _():
            drain(1 - slot)
        blk = zbuf.shape[0]
        zbuf[...] = jnp.zeros_like(zbuf)
        n_blocks = xs_ref.shape[0] // blk

        def zero_fill(wait):
            def per_e(e, _):
                _run_copies(pad_ref[n_exp + e], zbuf, 0, xs_ref, pad_ref[e], sems.at[2], blk // 2, wait)
                return 0
            lax.fori_loop(0, n_exp, per_e, 0)

            def per_blk(bi, _):
                cp = pltpu.make_async_copy(zbuf, xs_ref.at[pl.ds(bi * blk, blk)], sems.at[2])
                if wait:
                    cp.wait()
                else:
                    cp.start()
                return 0
            lax.fori_loop(pad_ref[2 * n_exp], n_blocks, per_blk, 0)
        zero_fill(False)
        zero_fill(True)


def _expert_block(t, n_exp):
    mean_rows = t * TOP_K // n_exp
    blk = EXPERT_BLOCK_MIN
    while blk < EXPERT_BLOCK_MAX and blk * 8 <= mean_rows:
        blk *= 2
    return blk


def _dispatch(pad, cnt_t, ls_t, dst_t, h, p_out, n_rows, n_exp, blk):
    t, d = h.shape
    tm = TM_MERGE
    meta = lambda: pl.BlockSpec((1, 1, LANES), lambda i, pd: (i, 0, 0), memory_space=pltpu.SMEM)
    grid_spec = pltpu.PrefetchScalarGridSpec(
        num_scalar_prefetch=1,
        grid=(t // tm,),
        in_specs=[meta(), meta(), meta(),
                  pl.BlockSpec((tm, d), lambda i, pd: (i, 0)),
                  pl.BlockSpec((tm, LANES), lambda i, pd: (i, 0))],
        out_specs=pl.BlockSpec(memory_space=pl.ANY),
        scratch_shapes=[pltpu.VMEM((2, _stage_rows(tm, n_exp) // SUBLANES, SUBLANES, d // 2), jnp.uint32),
                        pltpu.VMEM((blk // SUBLANES, SUBLANES, d // 2), jnp.uint32),
                        pltpu.SMEM((2,), I32),
                        pltpu.SemaphoreType.DMA((3,))],
    )
    return pl.pallas_call(
        functools.partial(_dispatch_kernel, n_exp),
        out_shape=jax.ShapeDtypeStruct((n_rows // SUBLANES, SUBLANES, d // 2), jnp.uint32),
        grid_spec=grid_spec,
        compiler_params=_params(("arbitrary",)),
        name="dispatch",
    )(pad, cnt_t, ls_t, dst_t, h, p_out)


def _experts_kernel(be_ref, used_ref, xs_ref, wg_ref, wu_ref, wd_ref, ys_ref):
    del be_ref

    @pl.when(pl.program_id(0) >= used_ref[0])
    def _():
        ys_ref[...] = jnp.zeros_like(ys_ref)

    @pl.when(pl.program_id(0) < used_ref[0])
    def _():
        half = xs_ref.shape[1]
        lo, hi = _unpack(xs_ref[...])
        gate = _dot(lo, wg_ref[0, :half, :]) + _dot(hi, wg_ref[0, half:, :])
        up = _dot(lo, wu_ref[0, :half, :]) + _dot(hi, wu_ref[0, half:, :])
        ys_ref[...] = _pack(_dot((_silu(gate) * up).astype(BF16), wd_ref[0]))


def _experts(blk_expert, n_used, xs, wg, wu, wd, blk):
    n_rows, half = xs.shape
    d, de = wg.shape[1], wg.shape[2]
    row = lambda i, be, used: (jnp.minimum(i, used[0] - 1), 0)
    wsel = lambda i, be, used: (be[jnp.minimum(i, used[0] - 1)], 0, 0)
    grid_spec = pltpu.PrefetchScalarGridSpec(
        num_scalar_prefetch=2,
        grid=(n_rows // blk,),
        in_specs=[pl.BlockSpec((blk, half), row),
                  pl.BlockSpec((1, d, de), wsel),
                  pl.BlockSpec((1, d, de), wsel),
                  pl.BlockSpec((1, de, d), wsel)],
        out_specs=pl.BlockSpec((blk, half), lambda i, be, used: (i, 0)),
    )
    return pl.pallas_call(
        _experts_kernel,
        out_shape=jax.ShapeDtypeStruct((n_rows, half), jnp.uint32),
        grid_spec=grid_spec,
        compiler_params=_params(("arbitrary",)),
        name="experts",
    )(blk_expert, n_used, xs, wg, wu, wd)


def _combine_kernel(n_exp, cnt_ref, ls_ref, src_ref, cnt1_ref, ls1_ref, src1_ref, base_ref, p_ref,
                    w_ref, mod_ref, fg_ref, ys_ref, o_ref, stage, acc, sems):
    i = pl.program_id(0)
    n = pl.num_programs(0)
    tm = base_ref.shape[0]
    tpc = tm // SUBLANES
    pieces = stage.shape[1] // tpc
    half = stage.shape[3]
    top = 1 << (stage.shape[1].bit_length() - 1)
    slot = i % 2
    total = ls_ref[0, 0, n_exp]

    def gather(c_ref, l_ref, s_ref, sl):
        def per_expert(e, _):
            _run_copies(c_ref[0, 0, e], ys_ref, s_ref[0, 0, e], stage.at[sl], l_ref[0, 0, e],
                        sems.at[sl], tpc)
            return 0
        lax.fori_loop(0, n_exp, per_expert, 0)

    @pl.when(i == 0)
    def _():
        stage[...] = jnp.zeros_like(stage)
        gather(cnt_ref, ls_ref, src_ref, 0)

    @pl.when(i + 1 < n)
    def _():
        gather(cnt1_ref, ls1_ref, src1_ref, 1 - slot)

    _run_copies(total, ys_ref, 0, stage.at[slot], 0, sems.at[slot], top, wait=True)

    pt = p_ref[...].astype(F32).T
    wt = w_ref[...].T

    r = _row_index_bf16(tm)

    def add_rows(c):
        wb = jnp.zeros((tm, tm), BF16)
        for kk in range(TOP_K):
            wb = jnp.where(r == (pt[kk:kk + 1, :] - float(c * tm)).astype(BF16),
                           wt[kk:kk + 1, :].astype(BF16), wb)
        lo, hi = _unpack(stage[slot, c * tpc:(c + 1) * tpc].reshape(tm, half))
        tdot = lambda x: lax.dot_general(wb, x, (((0,), (0,)), ((), ())), preferred_element_type=F32)
        if c == 0:
            acc[:, :half] = tdot(lo)
            acc[:, half:] = tdot(hi)
        else:
            acc[:, :half] += tdot(lo)
            acc[:, half:] += tdot(hi)

    for c in range(pieces):
        if c < TOP_K:
            add_rows(c)
        else:
            pl.when(c * tpc < total)(functools.partial(add_rows, c))
    y = base_ref[...] + mod_ref[0, 5:6, :] * acc[...]
    o_ref[...] = _rms(y) * fg_ref[...]


def _combine(cnt_t, ls_t, src_t, base, p_out, w8, mod, final_g, ys, seq, n_exp):
    t, d = base.shape
    tm = TM_MERGE
    steps = t // tm
    per_seq = seq // tm
    n6 = mod.shape[1]
    cur = lambda: pl.BlockSpec((1, 1, LANES), lambda i: (i, 0, 0), memory_space=pltpu.SMEM)
    nxt = lambda: pl.BlockSpec((1, 1, LANES), lambda i: (jnp.minimum(i + 1, steps - 1), 0, 0),
                               memory_space=pltpu.SMEM)
    return pl.pallas_call(
        functools.partial(_combine_kernel, n_exp),
        out_shape=jax.ShapeDtypeStruct((t, d), F32),
        grid=(steps,),
        in_specs=[cur(), cur(), cur(), nxt(), nxt(), nxt(),
                  pl.BlockSpec((tm, d), lambda i: (i, 0)),
                  pl.BlockSpec((tm, LANES), lambda i: (i, 0)),
                  pl.BlockSpec((tm, LANES), lambda i: (i, 0)),
                  pl.BlockSpec((1, n6, d), lambda i: (i // per_seq, 0, 0)),
                  pl.BlockSpec((1, d), lambda i: (0, 0)),
                  pl.BlockSpec(memory_space=pl.ANY)],
        out_specs=pl.BlockSpec((tm, d), lambda i: (i, 0)),
        scratch_shapes=[pltpu.VMEM((2, _stage_rows(tm, n_exp) // SUBLANES, SUBLANES, d // 2), jnp.uint32),
                        pltpu.VMEM((tm, d), F32),
                        pltpu.SemaphoreType.DMA((2,))],
        compiler_params=_params(("arbitrary",)),
        name="combine",
    )(cnt_t, ls_t, src_t, cnt_t, ls_t, src_t, base, p_out, w8, mod, final_g.reshape(1, d), ys)


def _prepare(l, ada_w, ada_b, norm1_g, w_in, conv_w, conv_b, lru_wa, lru_ba, lru_wi, lru_bi,
             lru_lambda, gla_w2, gla_b2, gla_norm_g, proj_lru, proj_gla, merge_b, w_out, norm2_g,
             router_w, router_bias, exp_w_gate, exp_w_up, exp_w_down, sh_w_gate, sh_w_up,
             sh_w_down):
    d = w_in.shape[1]
    d_rnn = conv_w.shape[2]
    dkt = gla_w2.shape[3]
    dvt = gla_norm_g.shape[1]
    rank = gla_w2.shape[2]
    sizes = (d_rnn, d_rnn, dkt, dkt, dvt, dvt, 2 * rank, d, d)
    wb = w_in[l].astype(BF16)
    parts, off = [], 0
    for sz in sizes:
        parts.append(wb[:, off:off + sz])
        off += sz
    w2 = gla_w2[l].astype(BF16)
    zeros = jnp.zeros_like(w2[0])
    w2p = jnp.stack([jnp.concatenate([w2[0], zeros], axis=0),
                     jnp.concatenate([zeros, w2[1]], axis=0)])
    return dict(
        ada_w=ada_w[l], ada_b=ada_b[l], norm1_g=norm1_g[l], w_parts=parts,
        conv_w=conv_w[l], conv_b=conv_b[l], wa=lru_wa[l].astype(BF16), wi=lru_wi[l].astype(BF16),
        ba=lru_ba[l], bi=lru_bi[l], lam=lru_lambda[l], w2p=w2p, b2=gla_b2[l],
        gnorm=gla_norm_g[l], proj_lru=proj_lru[l].astype(BF16), proj_gla=proj_gla[l].astype(BF16),
        merge_b=merge_b[l], w_out=w_out[l].astype(BF16), norm2_g=norm2_g[l],
        router_w=router_w[l], router_bias=router_bias[l],
        wg=exp_w_gate[l].astype(BF16), wu=exp_w_up[l].astype(BF16), wd=exp_w_down[l].astype(BF16),
        sg=sh_w_gate[l].astype(BF16), su=sh_w_up[l].astype(BF16), sd=sh_w_down[l].astype(BF16))


def _layer(x, c, p, final_g):
    bsz, s, d = x.shape
    t = bsz * s
    x2 = x.reshape(t, d)
    mod = _ada(c, p["ada_w"], p["ada_b"])
    xr, yr, q, k, v, r, lr, gl, gg = _inproj(x2, mod, p["norm1_g"], p["w_parts"], s)
    seqv = lambda a: a.reshape(bsz, s, a.shape[1])
    y_lru = _lru(seqv(xr), seqv(yr), p["conv_w"], p["conv_b"], p["wa"], p["wi"], p["ba"], p["bi"],
                 p["lam"]).reshape(t, -1)
    y_gla = _gla(seqv(q), seqv(k), seqv(v), seqv(r), seqv(lr), p["w2p"], p["b2"],
                 p["gnorm"]).reshape(t, -1)
    base, h, p_out, w8, cnt_t, ls_t, rb_t, counts = _merge(
        x2, y_lru, y_gla, gl, gg, mod, p["norm2_g"], p["merge_b"], p["proj_lru"], p["proj_gla"],
        p["w_out"], p["router_w"], p["router_bias"], p["sg"], p["su"], p["sd"], s)
    n_exp = p["router_w"].shape[1]
    blk = _expert_block(t, n_exp)
    n_blocks = -(-(t // TM_MERGE) * _stage_rows(TM_MERGE, n_exp) // blk) + n_exp
    cnt = counts[0, :n_exp]
    padded = (cnt + blk - 1) // blk * blk
    pend = jnp.cumsum(padded)
    pstart = (pend - padded).astype(I32)
    blk_expert = jnp.minimum(
        jnp.sum((pend[None, :] <= (jnp.arange(n_blocks, dtype=I32) * blk)[:, None]).astype(I32), axis=1),
        n_exp - 1).astype(I32)
    row_t = rb_t + jnp.pad(pstart // SUBLANES, (0, LANES - n_exp))[None, None, :]
    n_used = (pend[-1:] // blk).astype(I32)
    pad = jnp.concatenate([(pstart + cnt) // SUBLANES, (padded - cnt) // SUBLANES, n_used]).astype(I32)
    xs = _dispatch(pad, cnt_t, ls_t, row_t, h, p_out, n_blocks * blk, n_exp, blk)
    ys = _experts(blk_expert, n_used, xs.reshape(n_blocks * blk, -1), p["wg"], p["wu"], p["wd"], blk)
    return _combine(cnt_t, ls_t, row_t, base, p_out, w8, mod, final_g, ys.reshape(xs.shape), s,
                    n_exp).reshape(bsz, s, d)


def kernel(x_prompt, x_sample, c_prompt, c_sample, ada_w, ada_b, norm1_g, w_in, conv_w, conv_b,
           lru_wa, lru_ba, lru_wi, lru_bi, lru_lambda, gla_w2, gla_b2, gla_norm_g, proj_lru,
           proj_gla, merge_b, w_out, norm2_g, router_w, router_bias, exp_w_gate, exp_w_up,
           exp_w_down, sh_w_gate, sh_w_up, sh_w_down, final_g):
    depth = ada_w.shape[0]
    assert depth == 1, "the fused final RMSNorm assumes a single layer"
    p = _prepare(0, ada_w, ada_b, norm1_g, w_in, conv_w, conv_b, lru_wa, lru_ba, lru_wi, lru_bi,
                 lru_lambda, gla_w2, gla_b2, gla_norm_g, proj_lru, proj_gla, merge_b, w_out,
                 norm2_g, router_w, router_bias, exp_w_gate, exp_w_up, exp_w_down, sh_w_gate,
                 sh_w_up, sh_w_down)
    return (_layer(x_prompt, c_prompt, p, final_g), _layer(x_sample, c_sample, p, final_g))
```

```python
import functools

import jax
import jax.numpy as jnp
from jax import lax
from jax.experimental import pallas as pl
from jax.experimental.pallas import tpu as pltpu

F32 = jnp.float32
BF16 = jnp.bfloat16
I32 = jnp.int32

EPS = 1e-6
LOG2_E = 1.4426950408889634
LRU_C = 8.0
GLA_HEADS = 4
GLA_TAU = 16.0
GLA_CHUNK = 64
N_GROUPS = 8
TOPK_GROUPS = 4
TOP_K = 8
ROUTED_SCALE = 2.5

LANES = 128
SUBLANES = 8
VMEM_LIMIT = 56 * 1024 * 1024

TM_INPROJ = 512
TM_MERGE = 256
EXPERT_BLOCK_MIN = 256
EXPERT_BLOCK_MAX = 1024
LRU_CHUNK = 256
CONV_PAD = 16


def _dot(a, b):
    return jnp.dot(a, b, preferred_element_type=F32)


def _split_bf16(a):
    hi = a.astype(BF16)
    lo = (a - hi.astype(F32)).astype(BF16)
    return hi, lo


def _dot3(a, b):
    a_hi, a_lo = _split_bf16(a)
    b_hi, b_lo = _split_bf16(b)
    return _dot(a_hi, b_hi) + _dot(a_lo, b_hi) + _dot(a_hi, b_lo)


def _sigmoid(x):
    return 1.0 / (1.0 + jnp.exp(-x))


def _silu(x):
    return x * _sigmoid(x)


def _softplus(z):
    return jnp.maximum(z, 0.0) + jnp.log(1.0 + jnp.exp(-jnp.abs(z)))


def _gelu_tanh(x):
    return 0.5 * x * (1.0 + jnp.tanh(0.7978845608028654 * (x + 0.044715 * (x * x * x))))


def _rms(x):
    return x * lax.rsqrt(jnp.mean(x * x, axis=-1, keepdims=True) + EPS)


def _const_spec(shape):
    zeros = (0,) * len(shape)
    return pl.BlockSpec(shape, lambda *_: zeros, pipeline_mode=pl.Buffered(1))


def _params(sem, vmem=VMEM_LIMIT):
    return pltpu.CompilerParams(dimension_semantics=sem, vmem_limit_bytes=vmem)


def _ada_kernel(c_ref, w_ref, b_ref, o_ref):
    o_ref[...] = _dot3(_silu(c_ref[...]), w_ref[...]) + b_ref[...]


def _ada(c, ada_w, ada_b):
    nb, d = c.shape
    nbp = -(-nb // SUBLANES) * SUBLANES
    cp = jnp.pad(c, ((0, nbp - nb), (0, 0)))
    n6 = ada_w.shape[1] // d
    out = pl.pallas_call(
        _ada_kernel,
        out_shape=jax.ShapeDtypeStruct((nbp, n6 * d), F32),
        grid=(n6,),
        in_specs=[pl.BlockSpec((nbp, d), lambda j: (0, 0)),
                  pl.BlockSpec((d, d), lambda j: (0, j)),
                  pl.BlockSpec((1, d), lambda j: (0, j))],
        out_specs=pl.BlockSpec((nbp, d), lambda j: (0, j)),
        compiler_params=_params(("arbitrary",)),
        name="ada",
    )(cp, ada_w, ada_b.reshape(1, -1))
    return out.reshape(nbp, n6, d)


def _inproj_kernel(x_ref, mod_ref, g_ref, *refs):
    n = len(refs) // 2
    w_refs, o_refs = refs[:n], refs[n:]
    h = _rms(x_ref[...]) * g_ref[...]
    h = h * (1.0 + mod_ref[0, 1:2, :]) + mod_ref[0, 0:1, :]
    hb = h.astype(BF16)
    for w_ref, o_ref in zip(w_refs, o_refs):
        o_ref[...] = _dot(hb, w_ref[...]).astype(o_ref.dtype)


def _inproj(x2, mod, norm_g, w_parts, seq):
    t, d = x2.shape
    tm = TM_INPROJ
    per_seq = seq // tm
    n6 = mod.shape[1]
    in_specs = [pl.BlockSpec((tm, d), lambda i: (i, 0)),
                pl.BlockSpec((1, n6, d), lambda i: (i // per_seq, 0, 0)),
                _const_spec((1, d))]
    in_specs += [_const_spec(w.shape) for w in w_parts]
    out_shape = [jax.ShapeDtypeStruct((t, w.shape[1]), BF16) for w in w_parts]
    out_specs = [pl.BlockSpec((tm, w.shape[1]), lambda i: (i, 0)) for w in w_parts]
    return pl.pallas_call(
        _inproj_kernel,
        out_shape=out_shape,
        grid=(t // tm,),
        in_specs=in_specs,
        out_specs=out_specs,
        compiler_params=_params(("arbitrary",)),
        name="inproj",
    )(x2, mod, norm_g.reshape(1, d), *w_parts)


def _scan_chunk(a, u, h, reverse, store):
    groups = a.shape[0] // SUBLANES
    a = a.reshape(groups, SUBLANES, a.shape[1])
    u = u.reshape(groups, SUBLANES, u.shape[1])
    rm = lax.broadcasted_iota(I32, a.shape, 1)
    for d in (1, 2, 4):
        shift = SUBLANES - d if reverse else d
        keep = (rm < SUBLANES - d) if reverse else (rm >= d)
        u = a * jnp.where(keep, pltpu.roll(u, shift, 1), 0.0) + u
        a = a * jnp.where(keep, pltpu.roll(a, shift, 1), 1.0)
    order = range(groups - 1, -1, -1) if reverse else range(groups)
    for g in order:
        hg = u[g] + a[g] * h
        store(g, hg)
        h = hg[0:1] if reverse else hg[SUBLANES - 1:SUBLANES]
    return h


def _lru_kernel(xr_ref, yr_ref, cw_ref, cb_ref, wa_ref, wi_ref, ba_ref, bi_ref, lam_ref,
                o_ref, xpad, xc, acc_ref):
    s = xr_ref.shape[1]
    width = xr_ref.shape[2]
    ch = LRU_CHUNK
    n = s // ch
    taps = cw_ref.shape[0]
    left = taps // 2
    zpad = jnp.zeros((CONV_PAD, width), xpad.dtype)
    xpad[0:CONV_PAD, :] = zpad
    xpad[CONV_PAD + s:CONV_PAD + s + CONV_PAD, :] = zpad

    def fill(c, _):
        r0 = pl.multiple_of(c * ch, ch)
        xpad[pl.ds(CONV_PAD + r0, ch), :] = xr_ref[0, pl.ds(r0, ch), :]
        return 0
    lax.fori_loop(0, n, fill, 0)

    def conv(c, _):
        r0 = pl.multiple_of(c * ch, ch)
        win = xpad[pl.ds(r0, ch + 2 * CONV_PAD), :].astype(F32)
        acc = jnp.zeros_like(win) + cb_ref[...]
        for i in range(taps):
            sh = (left - i) % (ch + 2 * CONV_PAD)
            src = win if sh == 0 else pltpu.roll(win, sh, 0)
            acc = acc + src * cw_ref[i:i + 1, :]
        xc[pl.ds(r0, ch), :] = acc[CONV_PAD:CONV_PAD + ch]
        return 0
    lax.fori_loop(0, n, conv, 0)

    def gate(xb, w_ref, b_ref, d):
        z = jnp.concatenate([_dot(xb[:, j * LANES:(j + 1) * LANES], w_ref[d, j])
                             for j in range(width // LANES)], axis=1) + b_ref[d:d + 1, :]
        return 0.5 + 0.5 * jnp.tanh(z)

    def gates(r0, d):
        x = xc[pl.ds(r0, ch), :]
        xb = x.astype(BF16)
        r = gate(xb, wa_ref, ba_ref, d)
        i = gate(xb, wi_ref, bi_ref, d)
        a = jnp.exp2((-LRU_C * LOG2_E * _softplus(-lam_ref[d:d + 1, :])) * r)
        y = 1.0 - a * a
        u = jnp.where(y > 0.0, y * lax.rsqrt(y), 0.0) * (i * x)
        return a, u

    def step(c, carry, final):
        h_f, h_b = carry
        rows = (pl.multiple_of(c * ch, ch), pl.multiple_of((n - 1 - c) * ch, ch))
        new = []
        for d, (r0, h) in enumerate(zip(rows, (h_f, h_b))):
            a, u = gates(r0, d)

            def store(g, v, r0=r0):
                at = pl.ds(r0 + g * SUBLANES, SUBLANES)
                acc_ref[at, :] = v + acc_ref[at, :] if final else v
            new.append(_scan_chunk(a, u, h, bool(d), store))
        if final:
            for r0 in rows:
                y = yr_ref[0, pl.ds(r0, ch), :].astype(F32)
                o_ref[0, pl.ds(r0, ch), :] = (acc_ref[pl.ds(r0, ch), :]
                                              * _gelu_tanh(y)).astype(o_ref.dtype)
        return tuple(new)
    zero = jnp.zeros((1, width), F32)
    carry = lax.fori_loop(0, n // 2, functools.partial(step, final=False), (zero, zero))
    lax.fori_loop(n // 2, n, functools.partial(step, final=True), carry)


def _lru(xr, yr, conv_w, conv_b, wa, wi, ba, bi, lam):
    b, s, c = xr.shape
    per = 2 if (c // LANES) % 2 == 0 else 1
    width = per * LANES
    assert (s // LRU_CHUNK) % 2 == 0
    seq = lambda: pl.BlockSpec((1, s, width), lambda i, j: (i, 0, j))
    chan = lambda rows: pl.BlockSpec((rows, width), lambda i, j: (0, j))
    wspec = lambda: pl.BlockSpec((2, per, LANES, LANES), lambda i, j: (0, j, 0, 0))
    return pl.pallas_call(
        _lru_kernel,
        out_shape=jax.ShapeDtypeStruct((b, s, c), BF16),
        grid=(b, c // width),
        in_specs=[seq(), seq(), chan(conv_w.shape[0]), chan(1), wspec(), wspec(),
                  chan(2), chan(2), chan(2)],
        out_specs=seq(),
        scratch_shapes=[pltpu.VMEM((s + 2 * CONV_PAD, width), BF16),
                        pltpu.VMEM((s, width), F32),
                        pltpu.VMEM((s, width), F32)],
        compiler_params=_params(("arbitrary", "arbitrary")),
        name="lru",
    )(xr, yr, conv_w, conv_b.reshape(1, c), wa, wi, ba, bi, lam)


def _gla_kernel(q_ref, k_ref, v_ref, r_ref, lr_ref, w2_ref, b2_ref, gn_ref, o_ref, acc):
    s = q_ref.shape[1]
    dk = q_ref.shape[2]
    dv = v_ref.shape[2]
    ck = GLA_CHUNK
    n = s // ck
    half = n // 2
    scale = dk ** -0.5
    row = lax.broadcasted_iota(I32, (ck, ck), 0)
    col = lax.broadcasted_iota(I32, (ck, ck), 1)
    tri_f = (row >= col).astype(BF16)
    tri_b = (row <= col).astype(BF16)
    mask_f = row >= col
    mask_b = col > row
    lane2 = lax.broadcasted_iota(I32, (2 * ck, 2 * ck), 1)
    zeros_v = jnp.zeros((ck, dv), BF16)

    def finish(r0, o):
        y = _rms(o) * gn_ref[...]
        rr = r_ref[0, pl.ds(r0, ck), :].astype(F32)
        o_ref[0, pl.ds(r0, ck), :] = (y * _silu(rr)).astype(o_ref.dtype)

    unroll = max(u for u in (1, 2, 4, 8) if half % u == 0)

    def group(it, carry, final):
        states = list(carry)
        jobs = []
        for uu in range(unroll):
            i = it * unroll + uu
            jobs += [(i, 0), (n - 1 - i, 1)]
        rows = [pl.multiple_of(c * ck, ck) for c, _ in jobs]
        pre = [_dot(lr_ref[0, pl.ds(r0, ck), :], w2_ref[d]) + b2_ref[d:d + 1, :]
               for (_, d), r0 in zip(jobs, rows)]
        logb = []
        for (_, d), x in zip(jobs, pre):
            g_hi, g_lo = _split_bf16(-_softplus(-x) * (1.0 / GLA_TAU))
            tri = tri_b if d else tri_f
            logb.append(_dot(tri, g_hi) + _dot(tri, g_lo))
        q_dec, att, k_t, decay = [], [], [], []
        for (_, d), r0, b in zip(jobs, rows, logb):
            q = q_ref[0, pl.ds(r0, ck), :].astype(F32) * scale
            k = k_ref[0, pl.ds(r0, ck), :].astype(F32)
            qd = (q * jnp.exp(b)).astype(BF16)
            kd = (k * jnp.exp(-b)).astype(BF16)
            a = lax.dot_general(qd, kd, (((1,), (1,)), ((), ())), preferred_element_type=F32)
            b_edge = b[0:1] if d else b[ck - 1:ck]
            stacked_t = jnp.concatenate([k * jnp.exp(b_edge - b), b], axis=0).T
            edge_lane = ck if d else 2 * ck - 1
            q_dec.append(qd)
            att.append(jnp.where(mask_b if d else mask_f, a, 0.0).astype(BF16))
            decay.append(jnp.exp(stacked_t[:, edge_lane:edge_lane + 1]))
            k_t.append(jnp.where(lane2 < ck, stacked_t, 0.0).astype(BF16))
        intra, inject = [], []
        for r0, a, kt in zip(rows, att, k_t):
            v = v_ref[0, pl.ds(r0, ck), :]
            intra.append(_dot(a, v))
            inject.append(_dot(kt, jnp.concatenate([v, zeros_v], axis=0)))
        seen = []
        for (_, d), dc, inj in zip(jobs, decay, inject):
            seen.append(states[d].astype(BF16))
            states[d] = states[d] * dc + inj
        for r0, qd, st, o in zip(rows, q_dec, seen, intra):
            o = o + _dot(qd, st)
            if final:
                finish(r0, o + acc[pl.ds(r0, ck), :])
            else:
                acc[pl.ds(r0, ck), :] = o
        return tuple(states)

    zero = jnp.zeros((dk, dv), F32)
    carry = lax.fori_loop(0, half // unroll, functools.partial(group, final=False), (zero, zero))
    lax.fori_loop(half // unroll, n // unroll, functools.partial(group, final=True), carry)


def _gla(q, k, v, r, lr, w2p, b2, gnorm):
    b, s, dkt = q.shape
    dvt = v.shape[2]
    h = GLA_HEADS
    dk, dv = dkt // h, dvt // h
    nlr = lr.shape[2]
    kspec = lambda: pl.BlockSpec((1, s, dk), lambda i, j: (i, 0, j))
    vspec = lambda: pl.BlockSpec((1, s, dv), lambda i, j: (i, 0, j))
    return pl.pallas_call(
        _gla_kernel,
        out_shape=jax.ShapeDtypeStruct((b, s, dvt), BF16),
        grid=(b, h),
        in_specs=[kspec(), kspec(), vspec(), vspec(),
                  pl.BlockSpec((1, s, nlr), lambda i, j: (i, 0, 0)),
                  pl.BlockSpec((2, nlr, dk), lambda i, j: (0, 0, j)),
                  pl.BlockSpec((2, dk), lambda i, j: (0, j)),
                  pl.BlockSpec((1, dv), lambda i, j: (0, j))],
        out_specs=vspec(),
        scratch_shapes=[pltpu.VMEM((s, dv), F32)],
        compiler_params=_params(("arbitrary", "arbitrary")),
        name="gla",
    )(q, k, v, r, lr, w2p, b2, gnorm.reshape(1, dvt))


def _group_reduce(x, lane, op):
    for sft in (1, 2, 4):
        up = pltpu.roll(x, LANES - sft, 1)
        dn = pltpu.roll(x, sft, 1)
        x = op(x, jnp.where((lane & sft) == 0, up, dn))
    return x


def _route(scores, sel, n_exp):
    neg = -jnp.inf
    lane = lax.broadcasted_iota(I32, scores.shape, 1)
    lane_f = lane.astype(F32)
    per_group = n_exp // N_GROUPS
    assert per_group == SUBLANES and n_exp <= LANES
    valid = lane < n_exp
    v = jnp.where(valid, sel, neg)
    m1 = _group_reduce(v, lane, jnp.maximum)
    first = _group_reduce(jnp.where(v == m1, lane_f, float(LANES)), lane, jnp.minimum)
    m2 = _group_reduce(jnp.where(lane_f == first, neg, v), lane, jnp.maximum)
    gs = m1 + m2
    gidx = lane >> 3
    n_slots = LANES // per_group
    rank = jnp.zeros(scores.shape, F32)
    for kk in range(1, n_slots):
        other = pltpu.roll(gs, per_group * kk, 1)
        og = (gidx - kk) & (n_slots - 1)
        better = jnp.where(other > gs, 1.0, jnp.where(other == gs, jnp.where(og < gidx, 1.0, 0.0), 0.0))
        rank = rank + better
    w = jnp.where(rank < float(TOPK_GROUPS), v, neg)
    cols = []
    chosen = jnp.zeros(scores.shape, F32)
    for _ in range(TOP_K):
        m = jnp.max(w, axis=1, keepdims=True)
        j = jnp.min(jnp.where(w == m, lane_f, float(LANES)), axis=1, keepdims=True)
        pick = lane_f == j
        w = jnp.where(pick, neg, w)
        chosen = jnp.where(pick, 1.0, chosen)
        cols.append(j)
    return cols, chosen, lane_f


def _merge_kernel(n_exp, x_ref, yl_ref, yg_ref, gl_ref, gg_ref, mod_ref, n2_ref, mb_ref,
                  pl_ref, pg_ref, wo_ref, rw_ref, rb_ref, sg_ref, su_ref, sd_ref,
                  base_ref, h_ref, p_ref, w_ref, cnt_ref, ls_ref, rb4_ref, tot_ref, run, lg):
    tm = x_ref.shape[0]
    i = pl.program_id(0)

    @pl.when(i == 0)
    def _():
        run[...] = jnp.zeros_like(run)
        lg[...] = jnp.zeros_like(lg)

    prev_logits = lg[...]
    a = _dot(yl_ref[...], pl_ref[...])
    b = _dot(yg_ref[...], pg_ref[...])
    merged = (_sigmoid(gl_ref[...].astype(F32) + mb_ref[0:1, :]) * a
              + _sigmoid(gg_ref[...].astype(F32) + mb_ref[1:2, :]) * b)
    mix = _dot(merged.astype(BF16), wo_ref[...])
    x1 = x_ref[...] + mod_ref[0, 2:3, :] * mix
    h = _rms(x1) * n2_ref[...]
    h = h * (1.0 + mod_ref[0, 4:5, :]) + mod_ref[0, 3:4, :]
    hb = h.astype(BF16)
    h_ref[...] = hb
    shared = _dot(( _silu(_dot(hb, sg_ref[...])) * _dot(hb, su_ref[...]) ).astype(BF16), sd_ref[...])
    base_ref[...] = x1 + mod_ref[0, 5:6, :] * shared

    lg[...] = _dot3(h, rw_ref[...])

    scores = _sigmoid(prev_logits)
    cols, chosen, lane_f = _route(scores, scores + rb_ref[...], n_exp)
    picked = jnp.where(chosen > 0.0, scores, 0.0)
    cw = picked * (ROUTED_SCALE / jnp.sum(picked, axis=1, keepdims=True))
    chosen = chosen * jnp.where(i > 0, 1.0, 0.0)

    r = lax.broadcasted_iota(I32, (tm, tm), 0)
    c = lax.broadcasted_iota(I32, (tm, tm), 1)
    before = (c < r).astype(BF16)
    rank = _dot(before, chosen.astype(BF16))
    cnt = rank[tm - 1:tm, :] + chosen[tm - 1:tm, :]
    cnt8 = jnp.floor((cnt + (SUBLANES - 1)) * (1.0 / SUBLANES)) * SUBLANES
    lr_ = lax.broadcasted_iota(I32, (LANES, LANES), 0)
    lc_ = lax.broadcasted_iota(I32, (LANES, LANES), 1)
    lstart = _dot(jnp.broadcast_to(cnt8, (SUBLANES, LANES)).astype(BF16),
                  (lr_ < lc_).astype(BF16))[0:1]
    pos = rank + lstart
    per_tile = 1.0 / SUBLANES
    cnt_ref[0] = (cnt8 * per_tile).astype(I32)
    ls_ref[0] = (lstart * per_tile).astype(I32)
    rb4_ref[0] = (run[...] * per_tile).astype(I32)
    run[...] = run[...] + cnt8
    tot_ref[...] = run[...].astype(I32)

    p_out = jnp.zeros(scores.shape, F32)
    w_out = jnp.zeros(scores.shape, F32)
    for kk, j in enumerate(cols):
        hit = lane_f == j
        pk = jnp.sum(jnp.where(hit, pos, 0.0), axis=1, keepdims=True)
        wk = jnp.sum(jnp.where(hit, cw, 0.0), axis=1, keepdims=True)
        slot = lane_f == float(kk)
        p_out = jnp.where(slot, pk, p_out)
        w_out = jnp.where(slot, wk, w_out)
    p_ref[...] = p_out.astype(I32)
    w_ref[...] = w_out


def _merge(x2, y_lru, y_gla, gl, gg, mod, norm2_g, merge_b, proj_lru, proj_gla, w_out,
           router_w, router_bias, sh_gate, sh_up, sh_down, seq):
    t, d = x2.shape
    tm = TM_MERGE
    per_seq = seq // tm
    n6 = mod.shape[1]
    n_exp = router_w.shape[1]
    rw = jnp.pad(router_w, ((0, 0), (0, LANES - n_exp)))
    rb = jnp.pad(router_bias.reshape(1, n_exp), ((0, 0), (0, LANES - n_exp)))
    n_tiles = t // tm
    proj = lambda i: jnp.minimum(i, n_tiles - 1)
    routed = lambda i: jnp.maximum(i - 1, 0)
    tok = lambda width: pl.BlockSpec((tm, width), lambda i: (proj(i), 0))
    in_specs = [tok(d), tok(y_lru.shape[1]), tok(y_gla.shape[1]), tok(d), tok(d),
                pl.BlockSpec((1, n6, d), lambda i: (proj(i) // per_seq, 0, 0)),
                _const_spec((1, d)), _const_spec((2, d)),
                _const_spec(proj_lru.shape), _const_spec(proj_gla.shape), _const_spec(w_out.shape),
                _const_spec(rw.shape), _const_spec(rb.shape),
                _const_spec(sh_gate.shape), _const_spec(sh_up.shape), _const_spec(sh_down.shape)]
    meta = jax.ShapeDtypeStruct((n_tiles, 1, LANES), I32)
    meta_spec = lambda: pl.BlockSpec((1, 1, LANES), lambda i: (routed(i), 0, 0))
    plan = lambda: pl.BlockSpec((tm, LANES), lambda i: (routed(i), 0))
    out_shape = [jax.ShapeDtypeStruct((t, d), F32), jax.ShapeDtypeStruct((t, d), BF16),
                 jax.ShapeDtypeStruct((t, LANES), I32), jax.ShapeDtypeStruct((t, LANES), F32),
                 meta, meta, meta, jax.ShapeDtypeStruct((1, LANES), I32)]
    out_specs = [tok(d), tok(d), plan(), plan(), meta_spec(), meta_spec(), meta_spec(),
                 pl.BlockSpec((1, LANES), lambda i: (0, 0))]
    return pl.pallas_call(
        functools.partial(_merge_kernel, n_exp),
        out_shape=out_shape,
        grid=(n_tiles + 1,),
        in_specs=in_specs,
        out_specs=out_specs,
        scratch_shapes=[pltpu.VMEM((1, LANES), F32), pltpu.VMEM((tm, LANES), F32)],
        compiler_params=_params(("arbitrary",)),
        name="merge",
    )(x2, y_lru, y_gla, gl, gg, mod, norm2_g.reshape(1, d), merge_b, proj_lru, proj_gla, w_out,
      rw, rb, sh_gate, sh_up, sh_down)


def _run_copies(n, src_ref, src0, dst_ref, dst0, sem, max_tiles, wait=False):
    for b in range(max_tiles.bit_length() - 1, -1, -1):
        size = 1 << b

        @pl.when((n & size) != 0)
        def _(b=b, size=size):
            off = (n >> (b + 1)) << (b + 1)
            cp = pltpu.make_async_copy(src_ref.at[pl.ds(src0 + off, size)],
                                       dst_ref.at[pl.ds(dst0 + off, size)], sem)
            if wait:
                cp.wait()
            else:
                cp.start()


def _row_index_bf16(tm):
    assert tm <= 256
    return lax.broadcasted_iota(I32, (tm, tm), 0).astype(F32).astype(BF16)


def _pack(x):
    half = x.shape[1] // 2
    bits = lax.bitcast_convert_type(x.astype(BF16).astype(F32), jnp.uint32)
    return (bits[:, :half] >> 16) | (bits[:, half:] & jnp.uint32(0xFFFF0000))


def _pack_bf16_valued(x):
    half = x.shape[1] // 2
    bits = lax.bitcast_convert_type(x, jnp.uint32)
    return (bits[:, :half] >> 16) | bits[:, half:]


def _unpack(w):
    lo = lax.bitcast_convert_type(w << 16, F32)
    hi = lax.bitcast_convert_type(w & jnp.uint32(0xFFFF0000), F32)
    return lo.astype(BF16), hi.astype(BF16)


def _stage_rows(tm, n_exp):
    return -(-(TOP_K * tm + SUBLANES * n_exp) // tm) * tm


def _dispatch_kernel(n_exp, pad_ref, cnt_ref, ls_ref, dst_ref, h_ref, p_ref, xs_ref,
                     stage, zbuf, sent, sems):
    i = pl.program_id(0)
    n = pl.num_programs(0)
    tm = h_ref.shape[0]
    tpc = tm // SUBLANES
    pieces = stage.shape[1] // tpc
    top = 1 << (stage.shape[1].bit_length() - 1)
    slot = i % 2
    total = ls_ref[0, 0, n_exp]

    def drain(sl):
        _run_copies(sent[sl], stage.at[sl], 0, xs_ref, 0, sems.at[sl], top, wait=True)

    @pl.when(i >= 2)
    def _():
        drain(slot)

    pt = p_ref[...].astype(F32).T
    r = _row_index_bf16(tm)

    def sort_rows(c):
        sel = jnp.zeros((tm, tm), BF16)
        for kk in range(TOP_K):
            sel = jnp.where(r == (pt[kk:kk + 1, :] - float(c * tm)).astype(BF16), 1.0, sel)
        packed = _pack_bf16_valued(_dot(sel, h_ref[...]))
        stage[slot, c * tpc:(c + 1) * tpc] = packed.reshape(tpc, SUBLANES, packed.shape[1])

    for c in range(pieces):
        if c < TOP_K:
            sort_rows(c)
        else:
            pl.when(c * tpc < total)(functools.partial(sort_rows, c))

    def per_expert(e, _):
        _run_copies(cnt_ref[0, 0, e], stage.at[slot], ls_ref[0, 0, e], xs_ref, dst_ref[0, 0, e],
                    sems.at[slot], tpc)
        return 0
    lax.fori_loop(0, n_exp, per_expert, 0)
    sent[slot] = total

    @pl.when(i == n - 1)
    def _():
        drain(slot)

        @pl.when(n >= 2)
        def _():
            drain(1 - slot)
        blk = zbuf.shape[0]
        zbuf[...] = jnp.zeros_like(zbuf)
        n_blocks = xs_ref.shape[0] // blk

        def zero_fill(wait):
            def per_e(e, _):
                _run_copies(pad_ref[n_exp + e], zbuf, 0, xs_ref, pad_ref[e], sems.at[2], blk // 2, wait)
                return 0
            lax.fori_loop(0, n_exp, per_e, 0)

            def per_blk(bi, _):
                cp = pltpu.make_async_copy(zbuf, xs_ref.at[pl.ds(bi * blk, blk)], sems.at[2])
                if wait:
                    cp.wait()
                else:
                    cp.start()
                return 0
            lax.fori_loop(pad_ref[2 * n_exp], n_blocks, per_blk, 0)
        zero_fill(False)
        zero_fill(True)


def _expert_block(t, n_exp):
    mean_rows = t * TOP_K // n_exp
    blk = EXPERT_BLOCK_MIN
    while blk < EXPERT_BLOCK_MAX and blk * 8 <= mean_rows:
        blk *= 2
    return blk


def _dispatch(pad, cnt_t, ls_t, dst_t, h, p_out, n_rows, n_exp, blk):
    t, d = h.shape
    tm = TM_MERGE
    meta = lambda: pl.BlockSpec((1, 1, LANES), lambda i, pd: (i, 0, 0), memory_space=pltpu.SMEM)
    grid_spec = pltpu.PrefetchScalarGridSpec(
        num_scalar_prefetch=1,
        grid=(t // tm,),
        in_specs=[meta(), meta(), meta(),
                  pl.BlockSpec((tm, d), lambda i, pd: (i, 0)),
                  pl.BlockSpec((tm, LANES), lambda i, pd: (i, 0))],
        out_specs=pl.BlockSpec(memory_space=pl.ANY),
        scratch_shapes=[pltpu.VMEM((2, _stage_rows(tm, n_exp) // SUBLANES, SUBLANES, d // 2), jnp.uint32),
                        pltpu.VMEM((blk // SUBLANES, SUBLANES, d // 2), jnp.uint32),
                        pltpu.SMEM((2,), I32),
                        pltpu.SemaphoreType.DMA((3,))],
    )
    return pl.pallas_call(
        functools.partial(_dispatch_kernel, n_exp),
        out_shape=jax.ShapeDtypeStruct((n_rows // SUBLANES, SUBLANES, d // 2), jnp.uint32),
        grid_spec=grid_spec,
        compiler_params=_params(("arbitrary",)),
        name="dispatch",
    )(pad, cnt_t, ls_t, dst_t, h, p_out)


def _experts_kernel(be_ref, used_ref, xs_ref, wg_ref, wu_ref, wd_ref, ys_ref):
    del be_ref

    @pl.when(pl.program_id(0) >= used_ref[0])
    def _():
        ys_ref[...] = jnp.zeros_like(ys_ref)

    @pl.when(pl.program_id(0) < used_ref[0])
    def _():
        half = xs_ref.shape[1]
        lo, hi = _unpack(xs_ref[...])
        gate = _dot(lo, wg_ref[0, :half, :]) + _dot(hi, wg_ref[0, half:, :])
        up = _dot(lo, wu_ref[0, :half, :]) + _dot(hi, wu_ref[0, half:, :])
        ys_ref[...] = _pack(_dot((_silu(gate) * up).astype(BF16), wd_ref[0]))


def _experts(blk_expert, n_used, xs, wg, wu, wd, blk):
    n_rows, half = xs.shape
    d, de = wg.shape[1], wg.shape[2]
    row = lambda i, be, used: (jnp.minimum(i, used[0] - 1), 0)
    wsel = lambda i, be, used: (be[jnp.minimum(i, used[0] - 1)], 0, 0)
    grid_spec = pltpu.PrefetchScalarGridSpec(
        num_scalar_prefetch=2,
        grid=(n_rows // blk,),
        in_specs=[pl.BlockSpec((blk, half), row),
                  pl.BlockSpec((1, d, de), wsel),
                  pl.BlockSpec((1, d, de), wsel),
                  pl.BlockSpec((1, de, d), wsel)],
        out_specs=pl.BlockSpec((blk, half), lambda i, be, used: (i, 0)),
    )
    return pl.pallas_call(
        _experts_kernel,
        out_shape=jax.ShapeDtypeStruct((n_rows, half), jnp.uint32),
        grid_spec=grid_spec,
        compiler_params=_params(("arbitrary",)),
        name="experts",
    )(blk_expert, n_used, xs, wg, wu, wd)


def _combine_kernel(n_exp, cnt_ref, ls_ref, src_ref, cnt1_ref, ls1_ref, src1_ref, base_ref, p_ref,
                    w_ref, mod_ref, fg_ref, ys_ref, o_ref, stage, acc, sems):
    i = pl.program_id(0)
    n = pl.num_programs(0)
    tm = base_ref.shape[0]
    tpc = tm // SUBLANES
    pieces = stage.shape[1] // tpc
    half = stage.shape[3]
    top = 1 << (stage.shape[1].bit_length() - 1)
    slot = i % 2
    total = ls_ref[0, 0, n_exp]

    def gather(c_ref, l_ref, s_ref, sl):
        def per_expert(e, _):
            _run_copies(c_ref[0, 0, e], ys_ref, s_ref[0, 0, e], stage.at[sl], l_ref[0, 0, e],
                        sems.at[sl], tpc)
            return 0
        lax.fori_loop(0, n_exp, per_expert, 0)

    @pl.when(i == 0)
    def _():
        stage[...] = jnp.zeros_like(stage)
        gather(cnt_ref, ls_ref, src_ref, 0)

    @pl.when(i + 1 < n)
    def _():
        gather(cnt1_ref, ls1_ref, src1_ref, 1 - slot)

    _run_copies(total, ys_ref, 0, stage.at[slot], 0, sems.at[slot], top, wait=True)

    pt = p_ref[...].astype(F32).T
    wt = w_ref[...].T

    r = _row_index_bf16(tm)

    def add_rows(c):
        wb = jnp.zeros((tm, tm), BF16)
        for kk in range(TOP_K):
            wb = jnp.where(r == (pt[kk:kk + 1, :] - float(c * tm)).astype(BF16),
                           wt[kk:kk + 1, :].astype(BF16), wb)
        lo, hi = _unpack(stage[slot, c * tpc:(c + 1) * tpc].reshape(tm, half))
        tdot = lambda x: lax.dot_general(wb, x, (((0,), (0,)), ((), ())), preferred_element_type=F32)
        if c == 0:
            acc[:, :half] = tdot(lo)
            acc[:, half:] = tdot(hi)
        else:
            acc[:, :half] += tdot(lo)
            acc[:, half:] += tdot(hi)

    for c in range(pieces):
        if c < TOP_K:
            add_rows(c)
        else:
            pl.when(c * tpc < total)(functools.partial(add_rows, c))
    y = base_ref[...] + mod_ref[0, 5:6, :] * acc[...]
    o_ref[...] = _rms(y) * fg_ref[...]


def _combine(cnt_t, ls_t, src_t, base, p_out, w8, mod, final_g, ys, seq, n_exp):
    t, d = base.shape
    tm = TM_MERGE
    steps = t // tm
    per_seq = seq // tm
    n6 = mod.shape[1]
    cur = lambda: pl.BlockSpec((1, 1, LANES), lambda i: (i, 0, 0), memory_space=pltpu.SMEM)
    nxt = lambda: pl.BlockSpec((1, 1, LANES), lambda i: (jnp.minimum(i + 1, steps - 1), 0, 0),
                               memory_space=pltpu.SMEM)
    return pl.pallas_call(
        functools.partial(_combine_kernel, n_exp),
        out_shape=jax.ShapeDtypeStruct((t, d), F32),
        grid=(steps,),
        in_specs=[cur(), cur(), cur(), nxt(), nxt(), nxt(),
                  pl.BlockSpec((tm, d), lambda i: (i, 0)),
                  pl.BlockSpec((tm, LANES), lambda i: (i, 0)),
                  pl.BlockSpec((tm, LANES), lambda i: (i, 0)),
                  pl.BlockSpec((1, n6, d), lambda i: (i // per_seq, 0, 0)),
                  pl.BlockSpec((1, d), lambda i: (0, 0)),
                  pl.BlockSpec(memory_space=pl.ANY)],
        out_specs=pl.BlockSpec((tm, d), lambda i: (i, 0)),
        scratch_shapes=[pltpu.VMEM((2, _stage_rows(tm, n_exp) // SUBLANES, SUBLANES, d // 2), jnp.uint32),
                        pltpu.VMEM((tm, d), F32),
                        pltpu.SemaphoreType.DMA((2,))],
        compiler_params=_params(("arbitrary",)),
        name="combine",
    )(cnt_t, ls_t, src_t, cnt_t, ls_t, src_t, base, p_out, w8, mod, final_g.reshape(1, d), ys)


def _prepare(l, ada_w, ada_b, norm1_g, w_in, conv_w, conv_b, lru_wa, lru_ba, lru_wi, lru_bi,
             lru_lambda, gla_w2, gla_b2, gla_norm_g, proj_lru, proj_gla, merge_b, w_out, norm2_g,
             router_w, router_bias, exp_w_gate, exp_w_up, exp_w_down, sh_w_gate, sh_w_up,
             sh_w_down):
    d = w_in.shape[1]
    d_rnn = conv_w.shape[2]
    dkt = gla_w2.shape[3]
    dvt = gla_norm_g.shape[1]
    rank = gla_w2.shape[2]
    sizes = (d_rnn, d_rnn, dkt, dkt, dvt, dvt, 2 * rank, d, d)
    wb = w_in[l].astype(BF16)
    parts, off = [], 0
    for sz in sizes:
        parts.append(wb[:, off:off + sz])
        off += sz
    w2 = gla_w2[l].astype(BF16)
    zeros = jnp.zeros_like(w2[0])
    w2p = jnp.stack([jnp.concatenate([w2[0], zeros], axis=0),
                     jnp.concatenate([zeros, w2[1]], axis=0)])
    return dict(
        ada_w=ada_w[l], ada_b=ada_b[l], norm1_g=norm1_g[l], w_parts=parts,
        conv_w=conv_w[l], conv_b=conv_b[l],
        wa=(0.5 * lru_wa[l]).astype(BF16), wi=(0.5 * lru_wi[l]).astype(BF16),
        ba=0.5 * lru_ba[l], bi=0.5 * lru_bi[l], lam=lru_lambda[l], w2p=w2p, b2=gla_b2[l],
        gnorm=gla_norm_g[l], proj_lru=proj_lru[l].astype(BF16), proj_gla=proj_gla[l].astype(BF16),
        merge_b=merge_b[l], w_out=w_out[l].astype(BF16), norm2_g=norm2_g[l],
        router_w=router_w[l], router_bias=router_bias[l],
        wg=exp_w_gate[l].astype(BF16), wu=exp_w_up[l].astype(BF16), wd=exp_w_down[l].astype(BF16),
        sg=sh_w_gate[l].astype(BF16), su=sh_w_up[l].astype(BF16), sd=sh_w_down[l].astype(BF16))


def _layer(x, c, p, final_g):
    bsz, s, d = x.shape
    t = bsz * s
    x2 = x.reshape(t, d)
    mod = _ada(c, p["ada_w"], p["ada_b"])
    xr, yr, q, k, v, r, lr, gl, gg = _inproj(x2, mod, p["norm1_g"], p["w_parts"], s)
    seqv = lambda a: a.reshape(bsz, s, a.shape[1])
    y_lru = _lru(seqv(xr), seqv(yr), p["conv_w"], p["conv_b"], p["wa"], p["wi"], p["ba"], p["bi"],
                 p["lam"]).reshape(t, -1)
    y_gla = _gla(seqv(q), seqv(k), seqv(v), seqv(r), seqv(lr), p["w2p"], p["b2"],
                 p["gnorm"]).reshape(t, -1)
    base, h, p_out, w8, cnt_t, ls_t, rb_t, counts = _merge(
        x2, y_lru, y_gla, gl, gg, mod, p["norm2_g"], p["merge_b"], p["proj_lru"], p["proj_gla"],
        p["w_out"], p["router_w"], p["router_bias"], p["sg"], p["su"], p["sd"], s)
    n_exp = p["router_w"].shape[1]
    blk = _expert_block(t, n_exp)
    n_blocks = -(-(t // TM_MERGE) * _stage_rows(TM_MERGE, n_exp) // blk) + n_exp
    cnt = counts[0, :n_exp]
    padded = (cnt + blk - 1) // blk * blk
    pend = jnp.cumsum(padded)
    pstart = (pend - padded).astype(I32)
    blk_expert = jnp.minimum(
        jnp.sum((pend[None, :] <= (jnp.arange(n_blocks, dtype=I32) * blk)[:, None]).astype(I32), axis=1),
        n_exp - 1).astype(I32)
    row_t = rb_t + jnp.pad(pstart // SUBLANES, (0, LANES - n_exp))[None, None, :]
    n_used = (pend[-1:] // blk).astype(I32)
    pad = jnp.concatenate([(pstart + cnt) // SUBLANES, (padded - cnt) // SUBLANES, n_used]).astype(I32)
    xs = _dispatch(pad, cnt_t, ls_t, row_t, h, p_out, n_blocks * blk, n_exp, blk)
    ys = _experts(blk_expert, n_used, xs.reshape(n_blocks * blk, -1), p["wg"], p["wu"], p["wd"], blk)
    return _combine(cnt_t, ls_t, row_t, base, p_out, w8, mod, final_g, ys.reshape(xs.shape), s,
                    n_exp).reshape(bsz, s, d)


def kernel(x_prompt, x_sample, c_prompt, c_sample, ada_w, ada_b, norm1_g, w_in, conv_w, conv_b,
           lru_wa, lru_ba, lru_wi, lru_bi, lru_lambda, gla_w2, gla_b2, gla_norm_g, proj_lru,
           proj_gla, merge_b, w_out, norm2_g, router_w, router_bias, exp_w_gate, exp_w_up,
           exp_w_down, sh_w_gate, sh_w_up, sh_w_down, final_g):
    depth = ada_w.shape[0]
    assert depth == 1, "the fused final RMSNorm assumes a single layer"
    p = _prepare(0, ada_w, ada_b, norm1_g, w_in, conv_w, conv_b, lru_wa, lru_ba, lru_wi, lru_bi,
                 lru_lambda, gla_w2, gla_b2, gla_norm_g, proj_lru, proj_gla, merge_b, w_out,
                 norm2_g, router_w, router_bias, exp_w_gate, exp_w_up, exp_w_down, sh_w_gate,
                 sh_w_up, sh_w_down)
    return (_layer(x_prompt, c_prompt, p, final_g), _layer(x_sample, c_sample, p, final_g))
```
